```python
import math
import jax
import jax.numpy as jnp
from jax import lax
import numpy as np

D_MODEL = 1024
BATCH = 16
SEQ = 4096
DEPTH = 4

MEM_LEN = 256
MIX_WIDTH = D_MODEL // 2
N_BRANCH = 4
GDN_HEADS = 4
GDN_HEAD_DIM = MIX_WIDTH // GDN_HEADS
GDN_CHUNK = 64
GDN_CONV = 4
S5_GROUP = 16
S5_GROUPS = MIX_WIDTH // S5_GROUP
S5_STATE = 64
LRU_BLOCKS = 8
LRU_BLOCK = MIX_WIDTH // LRU_BLOCKS
LRU_CONV = 4
LRU_C = 8.0
SC_CONV = 3
XA_HEADS = 4
XA_HEAD_DIM = D_MODEL // XA_HEADS
D_FF = 128 * ((8 * D_MODEL // 3 + 127) // 128)
FFN_CONV = 3
EPS = 1e-6
IN_SIZES = (3 * MIX_WIDTH, MIX_WIDTH, 2 * GDN_HEADS, 2 * GDN_HEADS, MIX_WIDTH, MIX_WIDTH, MIX_WIDTH,
            MIX_WIDTH, MIX_WIDTH, MIX_WIDTH, N_BRANCH * D_MODEL)
IN_WIDTH = sum(IN_SIZES)

kernel_name = "hybrid_parallel_gated_bidir_encoder"


def _centred(width):
    return (width // 2, width - 1 - width // 2)


def rms_norm(x, gain):
    xf = x.astype(jnp.float32)
    y = xf * lax.rsqrt(jnp.mean(xf * xf, axis=-1, keepdims=True) + EPS)
    return (y * gain.astype(jnp.float32)).astype(x.dtype)


def dwconv(x, w):
    return lax.conv_general_dilated(
        x, w[:, None, :].astype(x.dtype), window_strides=(1,), padding=[_centred(w.shape[0])],
        dimension_numbers=("NWC", "WIO", "NWC"), feature_group_count=x.shape[-1])


def _l2norm(t):
    t = t.astype(jnp.float32)
    return t * lax.rsqrt(jnp.sum(t * t, axis=-1, keepdims=True) + EPS)


def _linear_scan(a, b, reverse):
    def combine(e1, e2):
        a1, b1 = e1
        a2, b2 = e2
        return a2 * a1, a2 * b1 + b2
    return lax.associative_scan(combine, (a, b), reverse=reverse, axis=1)[1]


def _complex_linear_scan(a_re, a_im, b_re, b_im, reverse):
    def combine(e1, e2):
        a1r, a1i, b1r, b1i = e1
        a2r, a2i, b2r, b2i = e2
        return (a2r * a1r - a2i * a1i, a2r * a1i + a2i * a1r,
                a2r * b1r - a2i * b1i + b2r, a2r * b1i + a2i * b1r + b2i)
    _, _, h_re, h_im = lax.associative_scan(combine, (a_re, a_im, b_re, b_im), reverse=reverse, axis=1)
    return h_re, h_im


def _gdn_chunked(q, k, v, log_a, beta):
    f32 = jnp.float32
    Bn, S, H, Dk = q.shape
    Dv = v.shape[-1]
    C = GDN_CHUNK
    N = S // C

    def blocks(t):
        t = t.astype(f32).reshape((Bn, N, C, H) + t.shape[3:])
        return jnp.moveaxis(t, 3, 1)

    q = blocks(q) * (Dk ** -0.5)
    k, v, g, b = blocks(k), blocks(v), blocks(log_a), blocks(beta)
    gam = jnp.cumsum(g, axis=-1)
    incl = jnp.tril(jnp.ones((C, C), dtype=bool))
    strict = jnp.tril(jnp.ones((C, C), dtype=bool), -1)
    diff = gam[..., :, None] - gam[..., None, :]
    decay = jnp.where(incl, jnp.exp(jnp.where(incl, diff, 0.0)), 0.0)
    kb = k * b[..., None]
    a_mat = jnp.eye(C, dtype=f32) + jnp.where(strict, jnp.einsum("bhncd,bhnmd->bhncm", kb, k) * decay, 0.0)
    rhs = jnp.concatenate([v * b[..., None], kb * jnp.exp(gam)[..., None]], axis=-1)
    wy = lax.linalg.triangular_solve(a_mat, rhs, left_side=True, lower=True, unit_diagonal=True)
    u, w = wy[..., :Dv], wy[..., Dv:]
    qk = jnp.einsum("bhncd,bhnmd->bhncm", q, k) * decay
    q_dec = q * jnp.exp(gam)[..., None]
    k_dec = k * jnp.exp(gam[..., -1:] - gam)[..., None]
    g_last = jnp.exp(gam[..., -1])

    def step(state, xs):
        q_c, k_c, u_c, w_c, qk_c, gl = xs
        v_new = u_c - jnp.einsum("bhcd,bhde->bhce", w_c, state)
        o = jnp.einsum("bhcd,bhde->bhce", q_c, state) + jnp.einsum("bhcm,bhme->bhce", qk_c, v_new)
        state = state * gl[..., None, None] + jnp.einsum("bhcd,bhce->bhde", k_c, v_new)
        return state, o

    xs = tuple(jnp.moveaxis(t, 2, 0) for t in (q_dec, k_dec, u, w, qk, g_last))
    _, o = lax.scan(step, jnp.zeros((Bn, H, Dk, Dv), f32), xs)
    o = jnp.moveaxis(o, 0, 2)
    return jnp.moveaxis(o, 1, 3).reshape(Bn, S, H, Dv)


def gdn_mixer(qkv, z, beta_logit, alpha, conv_w, a_log, dt_bias, out_gain):
    f32 = jnp.float32
    Bn, S, _ = qkv.shape
    qkv = jax.nn.silu(dwconv(qkv, conv_w))
    q, k, v = jnp.split(qkv, 3, axis=-1)
    heads = lambda t: t.reshape(Bn, S, GDN_HEADS, GDN_HEAD_DIM)
    q, k, v = _l2norm(heads(q)), _l2norm(heads(k)), heads(v).astype(f32)
    beta = jax.nn.sigmoid(beta_logit.astype(f32)).reshape(Bn, S, 2, GDN_HEADS)
    log_a = -jnp.exp(a_log.astype(f32)) * jax.nn.softplus(
        alpha.astype(f32).reshape(Bn, S, 2, GDN_HEADS) + dt_bias.astype(f32))
    flip = lambda t: jnp.flip(t, axis=1)
    o_fwd = _gdn_chunked(q, k, v, log_a[:, :, 0], beta[:, :, 0])
    o_bwd = flip(_gdn_chunked(flip(q), flip(k), flip(v), flip(log_a[:, :, 1]), flip(beta[:, :, 1])))
    o = o_fwd + o_bwd
    o = o * lax.rsqrt(jnp.mean(o * o, axis=-1, keepdims=True) + EPS) * out_gain.astype(f32)
    o = o.reshape(Bn, S, MIX_WIDTH) * jax.nn.silu(z.astype(f32))
    return o.astype(qkv.dtype)


def s5_mixer(u, lam_re, lam_im, log_step, b_re, b_im, c_re, c_im, d, glu_w, glu_b):
    f32 = jnp.float32
    Bn, S, W = u.shape
    uf = u.astype(f32)
    ug = uf.reshape(Bn, S, S5_GROUPS, S5_GROUP)
    lr = jnp.minimum(lam_re.astype(f32), -1e-4)
    li = lam_im.astype(f32)
    step = jnp.exp(log_step.astype(f32))[..., None]
    mag = jnp.exp(lr * step)
    abar_re, abar_im = mag * jnp.cos(li * step), mag * jnp.sin(li * step)
    den = lr * lr + li * li
    f_re = ((abar_re - 1.0) * lr + abar_im * li) / den
    f_im = (abar_im * lr - (abar_re - 1.0) * li) / den
    br, bi = b_re.astype(f32), b_im.astype(f32)
    bbar_re = f_re[..., None] * br - f_im[..., None] * bi
    bbar_im = f_re[..., None] * bi + f_im[..., None] * br
    cr, ci = c_re.astype(f32), c_im.astype(f32)
    y = uf * d.astype(f32)
    for di in range(2):
        bu_re = jnp.einsum("bsgj,gpj->bsgp", ug, bbar_re[di])
        bu_im = jnp.einsum("bsgj,gpj->bsgp", ug, bbar_im[di])
        a_re = jnp.broadcast_to(abar_re[di], (1, S) + abar_re.shape[1:])
        a_im = jnp.broadcast_to(abar_im[di], (1, S) + abar_im.shape[1:])
        h_re, h_im = _complex_linear_scan(a_re, a_im, bu_re, bu_im, reverse=di == 1)
        y_dir = jnp.einsum("bsgp,gjp->bsgj", h_re, cr[di]) - jnp.einsum("bsgp,gjp->bsgj", h_im, ci[di])
        y = y + y_dir.reshape(Bn, S, W)
    zg = jax.nn.gelu(y)
    out = zg * jax.nn.sigmoid(zg @ glu_w.astype(f32) + glu_b.astype(f32))
    return out.astype(u.dtype)


def rglru_mixer(xr, gate_in, conv_w, conv_b, wa, ba, wx, bx, lam):
    f32 = jnp.float32
    Bn, S, W = xr.shape
    xc = (dwconv(xr, conv_w) + conv_b.astype(xr.dtype)).astype(f32)
    xb = xc.reshape(Bn, S, LRU_BLOCKS, LRU_BLOCK)
    outs = []
    for di in range(2):
        r = jax.nn.sigmoid(jnp.einsum("bsnk,nkm->bsnm", xb, wa[di].astype(f32)).reshape(Bn, S, W) + ba[di].astype(f32))
        i = jax.nn.sigmoid(jnp.einsum("bsnk,nkm->bsnm", xb, wx[di].astype(f32)).reshape(Bn, S, W) + bx[di].astype(f32))
        log_a = -LRU_C * r * jax.nn.softplus(-lam[di].astype(f32))
        b = jnp.sqrt(-jnp.expm1(2.0 * log_a)) * (i * xc)
        outs.append(_linear_scan(jnp.exp(log_a), b, reverse=di == 1))
    h = outs[0] + outs[1]
    return (h * jax.nn.gelu(gate_in.astype(f32))).astype(xr.dtype)


def short_conv_mixer(b_gate, c_gate, xin, conv_w):
    return b_gate * dwconv(c_gate * xin, conv_w)


def cross_attention(h, mem_h, w_q, w_kv, w_o):
    Bn, S, D = h.shape
    M = mem_h.shape[1]
    q = (h @ w_q).reshape(Bn, S, XA_HEADS, XA_HEAD_DIM)
    k, v = jnp.split(mem_h @ w_kv, 2, axis=-1)
    k = k.reshape(Bn, M, XA_HEADS, XA_HEAD_DIM)
    v = v.reshape(Bn, M, XA_HEADS, XA_HEAD_DIM)
    s = jnp.einsum("bshd,bmhd->bhsm", q, k).astype(jnp.float32) * (XA_HEAD_DIM ** -0.5)
    p = jax.nn.softmax(s, axis=-1).astype(v.dtype)
    o = jnp.einsum("bhsm,bmhd->bshd", p, v).reshape(Bn, S, D)
    return o @ w_o


def conv_glu_ffn(h, w_up, conv_w, conv_b, w_down):
    u = dwconv(h @ w_up, conv_w) + conv_b.astype(h.dtype)
    gate, up = jnp.split(u, 2, axis=-1)
    return (jax.nn.silu(gate) * up) @ w_down


def setup_inputs(seed: int = 0) -> dict:
    key = jax.random.key(seed)
    keys = jax.random.split(key, 64)
    counter = [0]
    f32 = jnp.float32
    L, W, D, F = DEPTH, MIX_WIDTH, D_MODEL, D_FF
    G, P, J = S5_GROUPS, S5_STATE, S5_GROUP

    def nxt():
        counter[0] += 1
        return keys[counter[0] - 1]

    def nrm(shape, scale):
        return scale * jax.random.normal(nxt(), shape, f32)

    def gain(shape):
        return 1.0 + nrm(shape, 0.02)

    def unif(shape, lo, hi):
        return jax.random.uniform(nxt(), shape, f32, lo, hi)

    dt = jnp.exp(unif((L, 2, GDN_HEADS), math.log(1e-3), math.log(1e-1)))
    gdn_dt_bias = dt + jnp.log(-jnp.expm1(-dt))
    n_idx = jnp.arange(P, dtype=f32)
    a_c = unif((L, 2, W), 0.9, 0.999)
    a_base = a_c ** (1.0 / LRU_C)
    lru_lambda = jnp.log(a_base) - jnp.log1p(-a_base)
    return {
        "x": nrm((BATCH, SEQ, D), 1.0),
        "mem": nrm((BATCH, MEM_LEN, D), 1.0),
        "mix_norm": gain((L, D)),
        "w_in": nrm((L, D, IN_WIDTH), D ** -0.5),
        "gdn_conv": nrm((L, GDN_CONV, 3 * W), GDN_CONV ** -0.5),
        "gdn_a_log": jnp.log(unif((L, 2, GDN_HEADS), 1.0, 16.0)),
        "gdn_dt_bias": gdn_dt_bias,
        "gdn_out_norm": gain((L, GDN_HEAD_DIM)),
        "s5_lambda_re": -0.5 + nrm((L, 2, G, P), 0.01),
        "s5_lambda_im": math.pi * n_idx + nrm((L, 2, G, P), 0.01),
        "s5_log_step": unif((L, 2, G), math.log(1e-3), math.log(1e-1)),
        "s5_b_re": nrm((L, 2, G, P, J), (2 * J) ** -0.5),
        "s5_b_im": nrm((L, 2, G, P, J), (2 * J) ** -0.5),
        "s5_c_re": nrm((L, 2, G, J, P), P ** -0.5),
        "s5_c_im": nrm((L, 2, G, J, P), P ** -0.5),
        "s5_d": nrm((L, W), 1.0),
        "s5_glu_w": nrm((L, W, W), W ** -0.5),
        "s5_glu_b": nrm((L, W), 0.01),
        "lru_conv_w": nrm((L, LRU_CONV, W), LRU_CONV ** -0.5),
        "lru_conv_b": nrm((L, W), 0.01),
        "lru_gate_a_w": nrm((L, 2, LRU_BLOCKS, LRU_BLOCK, LRU_BLOCK), LRU_BLOCK ** -0.5),
        "lru_gate_a_b": nrm((L, 2, W), 0.01),
        "lru_gate_x_w": nrm((L, 2, LRU_BLOCKS, LRU_BLOCK, LRU_BLOCK), LRU_BLOCK ** -0.5),
        "lru_gate_x_b": nrm((L, 2, W), 0.01),
        "lru_lambda": lru_lambda,
        "sc_conv": nrm((L, SC_CONV, W), SC_CONV ** -0.5),
        "w_branch": nrm((L, N_BRANCH, W, D), W ** -0.5),
        "w_mix_out": nrm((L, D, D), D ** -0.5),
        "xa_norm": gain((L, D)),
        "xa_mem_norm": gain((L, D)),
        "xa_w_q": nrm((L, D, D), D ** -0.5),
        "xa_w_kv": nrm((L, D, 2 * D), D ** -0.5),
        "xa_w_o": nrm((L, D, D), D ** -0.5),
        "ffn_norm": gain((L, D)),
        "ffn_w_up": nrm((L, D, 2 * F), D ** -0.5),
        "ffn_conv_w": nrm((L, FFN_CONV, 2 * F), FFN_CONV ** -0.5),
        "ffn_conv_b": nrm((L, 2 * F), 0.01),
        "ffn_w_down": nrm((L, F, D), F ** -0.5),
        "final_norm": gain((D,)),
    }


def reference(x, mem, mix_norm, w_in, gdn_conv, gdn_a_log, gdn_dt_bias, gdn_out_norm,
              s5_lambda_re, s5_lambda_im, s5_log_step, s5_b_re, s5_b_im, s5_c_re, s5_c_im, s5_d,
              s5_glu_w, s5_glu_b, lru_conv_w, lru_conv_b, lru_gate_a_w, lru_gate_a_b, lru_gate_x_w,
              lru_gate_x_b, lru_lambda, sc_conv, w_branch, w_mix_out, xa_norm, xa_mem_norm, xa_w_q,
              xa_w_kv, xa_w_o, ffn_norm, ffn_w_up, ffn_conv_w, ffn_conv_b, ffn_w_down, final_norm):
    Bn, S, D = x.shape
    splits = [int(i) for i in np.cumsum(IN_SIZES)[:-1]]
    for l in range(DEPTH):
        h = rms_norm(x, mix_norm[l])
        (qkv, z, beta_logit, alpha, s5_u, lru_x, lru_g, sc_b, sc_c, sc_x, gate_logits) = jnp.split(
            h @ w_in[l], splits, axis=-1)
        ys = (
            gdn_mixer(qkv, z, beta_logit, alpha, gdn_conv[l], gdn_a_log[l], gdn_dt_bias[l], gdn_out_norm[l]),
            s5_mixer(s5_u, s5_lambda_re[l], s5_lambda_im[l], s5_log_step[l], s5_b_re[l], s5_b_im[l],
                     s5_c_re[l], s5_c_im[l], s5_d[l], s5_glu_w[l], s5_glu_b[l]),
            rglru_mixer(lru_x, lru_g, lru_conv_w[l], lru_conv_b[l], lru_gate_a_w[l], lru_gate_a_b[l],
                        lru_gate_x_w[l], lru_gate_x_b[l], lru_lambda[l]),
            short_conv_mixer(sc_b, sc_c, sc_x, sc_conv[l]),
        )
        gates = jax.nn.sigmoid(gate_logits.reshape(Bn, S, N_BRANCH, D))
        merged = gates[:, :, 0] * (ys[0] @ w_branch[l, 0])
        for m in range(1, N_BRANCH):
            merged = merged + gates[:, :, m] * (ys[m] @ w_branch[l, m])
        x = x + merged @ w_mix_out[l]
        x = x + cross_attention(rms_norm(x, xa_norm[l]), rms_norm(mem, xa_mem_norm[l]),
                                xa_w_q[l], xa_w_kv[l], xa_w_o[l])
        x = x + conv_glu_ffn(rms_norm(x, ffn_norm[l]), ffn_w_up[l], ffn_conv_w[l], ffn_conv_b[l], ffn_w_down[l])
    return rms_norm(x, final_norm)
```

```python
import functools
import math

import jax
import jax.numpy as jnp
from jax import lax
from jax.experimental import pallas as pl
from jax.experimental.pallas import tpu as pltpu

F32 = jnp.float32
BF16 = jnp.bfloat16
EPS = 1e-6

LANES = 128
BF16_ROWS = 16
VMEM_LIMIT = 56 * 1024 * 1024

GDN_HEADS = 4
GDN_CHUNK = 64
GDN_CONV = 4
S5_GROUP = 16
S5_STATE = 64
LRU_C = 8.0
XA_HEADS = 4

HI = lax.Precision.HIGHEST


def _cparams(sem):
    return pltpu.CompilerParams(dimension_semantics=sem, vmem_limit_bytes=VMEM_LIMIT)


def _rms(x, g):
    return x * lax.rsqrt(jnp.mean(x * x, axis=-1, keepdims=True) + EPS) * g


def _dot(a, b):
    return jnp.dot(a, b, preferred_element_type=F32)


def _dot_nt(a, b):
    return lax.dot_general(a, b, (((1,), (1,)), ((), ())), preferred_element_type=F32)


def _softplus(x):
    return jnp.maximum(x, 0.0) + jnp.log(1.0 + jnp.exp(-jnp.abs(x)))


def _const_spec(shape):
    nd = len(shape)
    return pl.BlockSpec(shape, lambda *_: (0,) * nd)


def _proj_kernel(x_ref, g_ref, w_ref, wab_ref, p_ref, ab_ref, h_ref):
    @pl.when(pl.program_id(1) == 0)
    def _():
        hb = _rms(x_ref[...], g_ref[...]).astype(BF16)
        h_ref[...] = hb
        ab_ref[...] = _dot(hb, wab_ref[...])

    p_ref[...] = _dot(h_ref[...], w_ref[...]).astype(BF16)


def _proj(x2, gain, w_main, w_ab, tm, tn):
    T, D = x2.shape
    N = w_main.shape[1]
    return pl.pallas_call(
        _proj_kernel,
        grid=(T // tm, N // tn),
        in_specs=[
            pl.BlockSpec((tm, D), lambda i, j: (i, 0)),
            pl.BlockSpec((1, D), lambda i, j: (0, 0)),
            pl.BlockSpec((D, tn), lambda i, j: (0, j)),
            pl.BlockSpec((D, LANES), lambda i, j: (0, 0)),
        ],
        out_specs=[
            pl.BlockSpec((tm, tn), lambda i, j: (i, j)),
            pl.BlockSpec((tm, LANES), lambda i, j: (i, 0)),
        ],
        out_shape=[jax.ShapeDtypeStruct((T, N), BF16), jax.ShapeDtypeStruct((T, LANES), F32)],
        scratch_shapes=[pltpu.VMEM((tm, D), BF16)],
        compiler_params=_cparams(("parallel", "arbitrary")),
        name="proj",
    )(x2, gain, w_main, w_ab)


def _qkvprep_kernel(xm_ref, xp_ref, xn_ref, cw_ref, o_ref, *, n_norm):
    i = pl.program_id(1)
    last = pl.num_programs(1) - 1
    ts = xm_ref.shape[1]
    hal = xp_ref.shape[1]
    keep_p = (i > 0).astype(F32)
    keep_n = (i < last).astype(F32)
    row = lax.broadcasted_iota(jnp.int32, (ts, 1), 0)
    for c in range(xm_ref.shape[2] // LANES):
        sl = slice(c * LANES, (c + 1) * LANES)
        x = xm_ref[0, :, sl].astype(F32)
        xp = xp_ref[0, :, sl].astype(F32) * keep_p
        xn = xn_ref[0, :, sl].astype(F32) * keep_n
        m1 = jnp.where(row == 0, xp[hal - 1:hal], pltpu.roll(x, 1, 0))
        m2 = pltpu.roll(x, 2, 0)
        m2 = jnp.where(row == 0, xp[hal - 2:hal - 1], m2)
        m2 = jnp.where(row == 1, xp[hal - 1:hal], m2)
        p1 = jnp.where(row == ts - 1, xn[0:1], pltpu.roll(x, ts - 1, 0))
        w = cw_ref[:, sl]
        y = w[0:1] * m2 + w[1:2] * m1 + w[2:3] * x + w[3:4] * p1
        y = y * jax.nn.sigmoid(y)
        if c < n_norm:
            y = y * lax.rsqrt(jnp.sum(y * y, axis=-1, keepdims=True) + EPS)
        o_ref[0, :, sl] = y.astype(o_ref.dtype)


def _qkvprep(P3, conv_w, ts):
    Bn, S, _ = P3.shape
    C = conv_w.shape[1]
    hal = BF16_ROWS
    nb = ts // hal
    return pl.pallas_call(
        functools.partial(_qkvprep_kernel, n_norm=2 * GDN_HEADS),
        grid=(Bn, S // ts),
        in_specs=[
            pl.BlockSpec((1, ts, C), lambda b, i: (b, i, 0)),
            pl.BlockSpec((1, hal, C), lambda b, i: (b, jnp.maximum(i * nb - 1, 0), 0)),
            pl.BlockSpec((1, hal, C), lambda b, i: (b, jnp.minimum((i + 1) * nb, S // hal - 1), 0)),
            _const_spec(conv_w.shape),
        ],
        out_specs=pl.BlockSpec((1, ts, C), lambda b, i: (b, i, 0)),
        out_shape=jax.ShapeDtypeStruct((Bn, S, C), BF16),
        compiler_params=_cparams(("parallel", "parallel")),
        name="gdn_prep",
    )(P3, P3, P3, conv_w)


def _gdn_chunk(d, qc, gam, beta, s_ref, tri_incl, tri_strict, eye, blk_id, lane_id):
    C = GDN_CHUNK
    H = GDN_HEADS
    Dh = LANES
    stack = lambda off: jnp.concatenate([qc[:, off + h * Dh: off + (h + 1) * Dh] for h in range(H)], axis=0)
    Qs = stack(0).astype(F32) * (Dh ** -0.5)
    Kb16 = stack(H * Dh)
    Ks = Kb16.astype(F32)
    Vs = stack(2 * H * Dh).astype(F32)
    gcol = jnp.concatenate(gam, axis=0)
    bcol = jnp.concatenate(beta, axis=0)
    edge = C - 1 if d == 0 else 0
    glast = [g[edge:edge + 1] for g in gam]
    glast_col = jnp.concatenate([jnp.broadcast_to(g, (C, 1)) for g in glast], axis=0)

    gb = jnp.broadcast_to(gcol, (H * C, LANES))
    A = jnp.where(lane_id == 0, gb, jnp.where(lane_id == 1, 1.0, 0.0))
    Bm = jnp.where(lane_id == 0, 1.0, jnp.where(lane_id == 1, -gb, 0.0))
    diff = lax.dot_general(A, Bm, (((1,), (1,)), ((), ())), precision=HI, preferred_element_type=F32)
    decay = jnp.where(tri_incl, jnp.exp(jnp.where(tri_incl, diff, 0.0)), 0.0)

    kb = Ks * bcol
    kk = _dot_nt(kb.astype(BF16), Kb16)
    X = jnp.where(tri_strict, -(kk * decay), 0.0)
    qk = _dot_nt(Qs.astype(BF16), Kb16) * decay

    P = eye + X
    for _ in range(5):
        X = jnp.dot(X, X, precision=HI, preferred_element_type=F32)
        P = P + jnp.dot(P, X, precision=HI, preferred_element_type=F32)

    egam = jnp.exp(gcol)
    rhs = jnp.concatenate([Vs * bcol, kb * egam], axis=1)
    wy = _dot(P.astype(BF16), rhs.astype(BF16))
    U, Wm = wy[:, :Dh], wy[:, Dh:]
    qdec = Qs * egam
    kdecT = (Ks * jnp.exp(glast_col - gcol)).T

    vnew, qS = [], []
    for h in range(H):
        rs = slice(h * C, (h + 1) * C)
        lhs = jnp.concatenate([Wm[rs], qdec[rs]], axis=0).astype(BF16)
        r = _dot(lhs, s_ref[h].astype(BF16))
        vnew.append(U[rs] - r[:C])
        qS.append(r[C:])
    vnew = jnp.concatenate(vnew, axis=0)
    vb = vnew.astype(BF16)
    out = jnp.concatenate(qS, axis=0) + _dot(qk.astype(BF16), vb)
    for h in range(H):
        kT = jnp.where(blk_id == h, kdecT, 0.0).astype(BF16)
        s_ref[h] = s_ref[h] * jnp.exp(glast[h]) + _dot(kT, vb)
    return out


def _gdn_kernel(qf_ref, af_ref, qb_ref, abk_ref, prm_ref, of_ref, ob_ref, sf_ref, sb_ref):
    C = GDN_CHUNK
    H = GDN_HEADS
    CT = qf_ref.shape[1]
    nch = CT // C

    @pl.when(pl.program_id(1) == 0)
    def _():
        sf_ref[...] = jnp.zeros_like(sf_ref)
        sb_ref[...] = jnp.zeros_like(sb_ref)

    r = lax.broadcasted_iota(jnp.int32, (H * C, H * C), 0)
    c = lax.broadcasted_iota(jnp.int32, (H * C, H * C), 1)
    same = (r // C) == (c // C)
    eye = (r == c).astype(F32)
    lane_id = lax.broadcasted_iota(jnp.int32, (H * C, LANES), 1)
    blk_id = lax.broadcasted_iota(jnp.int32, (LANES, H * C), 1) // C
    rt = lax.broadcasted_iota(jnp.int32, (CT, CT), 0)
    ct = lax.broadcasted_iota(jnp.int32, (CT, CT), 1)
    same_t = (rt // C) == (ct // C)
    neg_a = -jnp.exp(prm_ref[0:1, :])
    dt_b = prm_ref[1:2, :]

    dirs = ((0, qf_ref, af_ref, of_ref, sf_ref), (1, qb_ref, abk_ref, ob_ref, sb_ref))
    pre = []
    for d, q_ref, a_ref, o_ref, s_ref in dirs:
        ab = a_ref[0]
        bt = jax.nn.sigmoid(ab)
        la = neg_a * _softplus(ab + dt_b)
        cum = (same_t & ((ct <= rt) if d == 0 else (ct >= rt))).astype(F32)
        gam = jnp.dot(cum, la, precision=HI, preferred_element_type=F32)
        tri_incl = same & ((c <= r) if d == 0 else (c >= r))
        tri_strict = same & ((c < r) if d == 0 else (c > r))
        pre.append((bt, gam, tri_incl, tri_strict))

    for step in range(nch):
        for d, q_ref, a_ref, o_ref, s_ref in dirs:
            bt, gam, tri_incl, tri_strict = pre[d]
            ci = step if d == 0 else nch - 1 - step
            rows = slice(ci * C, (ci + 1) * C)
            gl = [gam[rows, 2 * H + d * H + h: 2 * H + d * H + h + 1] for h in range(H)]
            bl = [bt[rows, d * H + h: d * H + h + 1] for h in range(H)]
            out = _gdn_chunk(d, q_ref[0, rows, :], gl, bl, s_ref, tri_incl, tri_strict, eye, blk_id, lane_id)
            for h in range(H):
                o_ref[0, rows, h * LANES:(h + 1) * LANES] = out[h * C:(h + 1) * C]


def _gdn(qkv, ab3, prm, ct):
    Bn, S, C3 = qkv.shape
    W = C3 // 3
    nt = S // ct
    dh = W // GDN_HEADS
    fwd = lambda b, n: (b, n, 0)
    bwd = lambda b, n: (b, nt - 1 - n, 0)
    return pl.pallas_call(
        _gdn_kernel,
        grid=(Bn, nt),
        in_specs=[
            pl.BlockSpec((1, ct, C3), fwd),
            pl.BlockSpec((1, ct, LANES), fwd),
            pl.BlockSpec((1, ct, C3), bwd),
            pl.BlockSpec((1, ct, LANES), bwd),
            _const_spec(prm.shape),
        ],
        out_specs=[pl.BlockSpec((1, ct, W), fwd), pl.BlockSpec((1, ct, W), bwd)],
        out_shape=[jax.ShapeDtypeStruct((Bn, S, W), F32)] * 2,
        scratch_shapes=[pltpu.VMEM((GDN_HEADS, dh, dh), F32)] * 2,
        compiler_params=_cparams(("parallel", "arbitrary")),
        name="gdn",
    )(qkv, ab3, qkv, ab3, prm)


def _s5_kernel(uf_ref, ub_ref, bw_ref, cw_ref, a_ref, yf_ref, yb_ref, hs_ref, carry_ref, *, nb):
    R = uf_ref.shape[0]
    tt = R // nb
    nq = bw_ref.shape[1]
    half = bw_ref.shape[3] // 2

    @pl.when(pl.program_id(0) == 0)
    def _():
        carry_ref[...] = jnp.zeros_like(carry_ref)

    for d, (u_ref, y_ref) in enumerate(((uf_ref, yf_ref), (ub_ref, yb_ref))):
        for q in range(nq):
            hs_ref[d, :, 2 * half * q:2 * half * (q + 1)] = _dot(u_ref[:, q * LANES:(q + 1) * LANES], bw_ref[d, q])
        for q in range(nq):
            re = slice(2 * half * q, 2 * half * q + half)
            im = slice(2 * half * q + half, 2 * half * (q + 1))
            ar = a_ref[d, 0, :, q * half:(q + 1) * half]
            ai = a_ref[d, 1, :, q * half:(q + 1) * half]

            def body(s, carry, d=d, re=re, im=im, ar=ar, ai=ai):
                hr, hi = carry
                t = s if d == 0 else tt - 1 - s
                rows = pl.ds(pl.multiple_of(t * nb, nb), nb)
                nr = ar * hr - ai * hi + hs_ref[d, rows, re]
                ni = ar * hi + ai * hr + hs_ref[d, rows, im]
                hs_ref[d, rows, re] = nr
                hs_ref[d, rows, im] = ni
                return nr, ni

            hr, hi = lax.fori_loop(0, tt, body, (carry_ref[d, :, re], carry_ref[d, :, im]), unroll=4)
            carry_ref[d, :, re] = hr
            carry_ref[d, :, im] = hi
        for q in range(nq):
            hq = hs_ref[d, :, 2 * half * q:2 * half * (q + 1)].astype(BF16)
            y_ref[:, q * LANES:(q + 1) * LANES] = _dot(hq, cw_ref[d, q])


def _s5(utm, bw, cw, a_b, nb, rows, col_blk):
    TB = utm.shape[0]
    nq = bw.shape[1]
    W = nq * LANES
    nstate = bw.shape[3] * nq
    nt = TB // rows
    fwd = lambda n: (n, col_blk)
    bwd = lambda n: (nt - 1 - n, col_blk)
    return pl.pallas_call(
        functools.partial(_s5_kernel, nb=nb),
        grid=(nt,),
        in_specs=[
            pl.BlockSpec((rows, W), fwd),
            pl.BlockSpec((rows, W), bwd),
            _const_spec(bw.shape),
            _const_spec(cw.shape),
            _const_spec(a_b.shape),
        ],
        out_specs=[pl.BlockSpec((rows, W), lambda n: (n, 0)), pl.BlockSpec((rows, W), lambda n: (nt - 1 - n, 0))],
        out_shape=[jax.ShapeDtypeStruct((TB, W), F32)] * 2,
        scratch_shapes=[pltpu.VMEM((2, rows, nstate), F32), pltpu.VMEM((2, nb, nstate), F32)],
        compiler_params=_cparams(("arbitrary",)),
        name="s5_scan",
    )(utm, utm, bw, cw, a_b)


def _lru_kernel(xf_ref, xfp_ref, xfn_ref, xb_ref, xbp_ref, xbn_ref, cw_ref, cb_ref, wg_ref, bg_ref, lam_ref,
                of_ref, ob_ref, a_s, b_s, carry_ref, *, nb):
    n = pl.program_id(0)
    nt = pl.num_programs(0)
    R = xf_ref.shape[0]
    W = xf_ref.shape[1]
    tt = R // nb

    @pl.when(n == 0)
    def _():
        carry_ref[...] = jnp.zeros_like(carry_ref)

    streams = ((0, n, xf_ref, xfp_ref, xfn_ref, of_ref), (1, nt - 1 - n, xb_ref, xbp_ref, xbn_ref, ob_ref))
    for d, pos, xm_ref, xp_ref, xn_ref, o_ref in streams:
        xp = xp_ref[...].astype(F32) * (pos > 0).astype(F32)
        xn = xn_ref[...].astype(F32) * (pos < nt - 1).astype(F32)
        xe = jnp.concatenate([xp, xm_ref[...].astype(F32), xn], axis=0)
        w = cw_ref[...]
        xc = cb_ref[...] + sum(w[k:k + 1] * xe[k * nb:k * nb + R] for k in range(GDN_CONV))
        gates = _dot(xc.astype(BF16), wg_ref[d]) + bg_ref[d]
        rg = jax.nn.sigmoid(gates[:, :W])
        ig = jax.nn.sigmoid(gates[:, W:])
        a = jnp.exp(-LRU_C * rg * _softplus(-lam_ref[d]))
        a_s[d] = a
        b_s[d] = jnp.sqrt(1.0 - a * a) * (ig * xc)

        def body(s, h, d=d, o_ref=o_ref):
            t = s if d == 0 else tt - 1 - s
            rows = pl.ds(pl.multiple_of(t * nb, nb), nb)
            h = a_s[d, rows, :] * h + b_s[d, rows, :]
            o_ref[rows, :] = h
            return h

        carry_ref[d] = lax.fori_loop(0, tt, body, carry_ref[d], unroll=4)


def _lru(utm, conv_w, conv_b, wg, bg, lam, nb, rows, col_blk):
    TB = utm.shape[0]
    W = conv_w.shape[1]
    nt = TB // rows
    hp = 2 * nb
    hn = nb
    nbt = TB // nb

    def specs(pos):
        return [
            pl.BlockSpec((rows, W), lambda n: (pos(n), col_blk)),
            pl.BlockSpec((hp, W), lambda n: (jnp.maximum(pos(n) * (rows // hp) - 1, 0), col_blk)),
            pl.BlockSpec((hn, W), lambda n: (jnp.minimum((pos(n) + 1) * (rows // hn), nbt - 1), col_blk)),
        ]

    fwd = lambda n: n
    bwd = lambda n: nt - 1 - n
    return pl.pallas_call(
        functools.partial(_lru_kernel, nb=nb),
        grid=(nt,),
        in_specs=specs(fwd) + specs(bwd) + [_const_spec(conv_w.shape), _const_spec(conv_b.shape),
                                            _const_spec(wg.shape), _const_spec(bg.shape), _const_spec(lam.shape)],
        out_specs=[pl.BlockSpec((rows, W), lambda n: (n, 0)), pl.BlockSpec((rows, W), lambda n: (nt - 1 - n, 0))],
        out_shape=[jax.ShapeDtypeStruct((TB, W), F32)] * 2,
        scratch_shapes=[pltpu.VMEM((2, rows, W), F32), pltpu.VMEM((2, rows, W), F32), pltpu.VMEM((2, nb, W), F32)],
        compiler_params=_cparams(("arbitrary",)),
        name="lru_scan",
    )(utm, utm, utm, utm, utm, utm, conv_w, conv_b, wg, bg, lam)


def _post_kernel(u_ref, g_ref, yf_ref, yb_ref, hf_ref, hb_ref, d_ref, gw_ref, gb_ref, o_ref):
    W = u_ref.shape[1]
    y = u_ref[...].astype(F32) * d_ref[...] + yf_ref[...] + yb_ref[...]
    zg = jax.nn.gelu(y)
    o_ref[:, :W] = (zg * jax.nn.sigmoid(_dot(zg.astype(BF16), gw_ref[...]) + gb_ref[...])).astype(o_ref.dtype)
    o_ref[:, W:] = ((hf_ref[...] + hb_ref[...]) * jax.nn.gelu(g_ref[...].astype(F32))).astype(o_ref.dtype)


def _post(utm, yf, yb, hf, hb, s5_d, glu_w, glu_b, rows, u_blk, g_blk):
    TB, W = yf.shape
    row = lambda n: (n, 0)
    return pl.pallas_call(
        _post_kernel,
        grid=(TB // rows,),
        in_specs=[
            pl.BlockSpec((rows, W), lambda n: (n, u_blk)),
            pl.BlockSpec((rows, W), lambda n: (n, g_blk)),
            pl.BlockSpec((rows, W), row), pl.BlockSpec((rows, W), row),
            pl.BlockSpec((rows, W), row), pl.BlockSpec((rows, W), row),
            _const_spec(s5_d.shape), _const_spec(glu_w.shape), _const_spec(glu_b.shape),
        ],
        out_specs=pl.BlockSpec((rows, 2 * W), row),
        out_shape=jax.ShapeDtypeStruct((TB, 2 * W), BF16),
        compiler_params=_cparams(("parallel",)),
        name="s5_lru_post",
    )(utm, utm, yf, yb, hf, hb, s5_d, glu_w, glu_b)


def _merge_kernel(x_ref, of_ref, ob_ref, z_ref, sl_ref, scb_ref, scc_ref, scx_ref, ccp_ref, cxp_ref, ccn_ref,
                  cxn_ref, g0_ref, g1_ref, g2_ref, g3_ref, gn_ref, scw_ref, wb_ref, wo_ref, o_ref):
    i = pl.program_id(1)
    last = pl.num_programs(1) - 1
    tm = x_ref.shape[1]
    W = z_ref.shape[2]
    hal = ccp_ref.shape[1]

    o = of_ref[0] + ob_ref[0]
    z = z_ref[0].astype(F32)
    ys = []
    for h in range(GDN_HEADS):
        oh = o[:, h * LANES:(h + 1) * LANES]
        ys.append(oh * lax.rsqrt(jnp.mean(oh * oh, axis=-1, keepdims=True) + EPS) * gn_ref[...])
    y_gdn = jnp.concatenate(ys, axis=1) * (z * jax.nn.sigmoid(z))

    cx = scc_ref[0].astype(F32) * scx_ref[0].astype(F32)
    cx_p = (ccp_ref[0, hal - 1:hal].astype(F32) * cxp_ref[0, hal - 1:hal].astype(F32)) * (i > 0).astype(F32)
    cx_n = (ccn_ref[0, 0:1].astype(F32) * cxn_ref[0, 0:1].astype(F32)) * (i < last).astype(F32)
    row = lax.broadcasted_iota(jnp.int32, (tm, 1), 0)
    m1 = jnp.where(row == 0, cx_p, pltpu.roll(cx, 1, 0))
    p1 = jnp.where(row == tm - 1, cx_n, pltpu.roll(cx, tm - 1, 0))
    w = scw_ref[...]
    y_sc = scb_ref[0].astype(F32) * (w[0:1] * m1 + w[1:2] * cx + w[2:3] * p1)

    branches = (y_gdn.astype(BF16), sl_ref[0, :, :W], sl_ref[0, :, W:], y_sc.astype(BF16))
    merged = None
    for m, (y, g_ref) in enumerate(zip(branches, (g0_ref, g1_ref, g2_ref, g3_ref))):
        t = jax.nn.sigmoid(g_ref[0].astype(F32)) * _dot(y, wb_ref[m])
        merged = t if merged is None else merged + t
    o_ref[0] = x_ref[0] + _dot(merged.astype(BF16), wo_ref[...])


def _merge(x3, o_f, o_b, P3, sl3, gdn_gain, sc_w, w_branch, w_out, tm, blk):
    Bn, S, D = x3.shape
    W = o_f.shape[2]
    hal = BF16_ROWS
    nb = tm // hal
    tok = lambda b, i: (b, i, 0)
    col = lambda k: (lambda b, i: (b, i, k))
    prev = lambda k: (lambda b, i: (b, jnp.maximum(i * nb - 1, 0), k))
    nxt = lambda k: (lambda b, i: (b, jnp.minimum((i + 1) * nb, S // hal - 1), k))
    gate0 = blk["gates"] * W // D
    return pl.pallas_call(
        _merge_kernel,
        grid=(Bn, S // tm),
        in_specs=[
            pl.BlockSpec((1, tm, D), tok),
            pl.BlockSpec((1, tm, W), tok), pl.BlockSpec((1, tm, W), tok),
            pl.BlockSpec((1, tm, W), col(blk["z"])),
            pl.BlockSpec((1, tm, 2 * W), tok),
            pl.BlockSpec((1, tm, W), col(blk["sc_b"])),
            pl.BlockSpec((1, tm, W), col(blk["sc_c"])),
            pl.BlockSpec((1, tm, W), col(blk["sc_x"])),
            pl.BlockSpec((1, hal, W), prev(blk["sc_c"])), pl.BlockSpec((1, hal, W), prev(blk["sc_x"])),
            pl.BlockSpec((1, hal, W), nxt(blk["sc_c"])), pl.BlockSpec((1, hal, W), nxt(blk["sc_x"])),
            pl.BlockSpec((1, tm, D), col(gate0)), pl.BlockSpec((1, tm, D), col(gate0 + 1)),
            pl.BlockSpec((1, tm, D), col(gate0 + 2)), pl.BlockSpec((1, tm, D), col(gate0 + 3)),
            _const_spec(gdn_gain.shape), _const_spec(sc_w.shape), _const_spec(w_branch.shape),
            _const_spec(w_out.shape),
        ],
        out_specs=pl.BlockSpec((1, tm, D), tok),
        out_shape=jax.ShapeDtypeStruct((Bn, S, D), F32),
        compiler_params=_cparams(("parallel", "parallel")),
        name="merge",
    )(x3, o_f, o_b, P3, sl3, P3, P3, P3, P3, P3, P3, P3, P3, P3, P3, P3, gdn_gain, sc_w, w_branch, w_out)


def _kv_kernel(m_ref, g_ref, w_ref, o_ref):
    o_ref[0] = _dot(_rms(m_ref[0], g_ref[...]).astype(BF16), w_ref[...]).astype(o_ref.dtype)


def _kv(mem, gain, w_kv):
    Bn, M, D = mem.shape
    return pl.pallas_call(
        _kv_kernel,
        grid=(Bn,),
        in_specs=[pl.BlockSpec((1, M, D), lambda b: (b, 0, 0)), _const_spec(gain.shape), _const_spec(w_kv.shape)],
        out_specs=pl.BlockSpec((1, M, 2 * D), lambda b: (b, 0, 0)),
        out_shape=jax.ShapeDtypeStruct((Bn, M, 2 * D), BF16),
        compiler_params=_cparams(("parallel",)),
        name="xa_kv",
    )(mem, gain, w_kv)


def _xa_kernel(x_ref, kv_ref, g_ref, wq_ref, wo_ref, o_ref):
    D = x_ref.shape[2]
    dh = D // XA_HEADS
    x = x_ref[0]
    q = _dot(_rms(x, g_ref[...]).astype(BF16), wq_ref[...]).astype(BF16)
    outs = []
    for h in range(XA_HEADS):
        k = kv_ref[0, :, h * dh:(h + 1) * dh]
        v = kv_ref[0, :, D + h * dh:D + (h + 1) * dh]
        s = _dot_nt(q[:, h * dh:(h + 1) * dh], k) * (dh ** -0.5)
        e = jnp.exp(s - jnp.max(s, axis=-1, keepdims=True))
        p = e / jnp.sum(e, axis=-1, keepdims=True)
        outs.append(_dot(p.astype(BF16), v))
    o = jnp.concatenate(outs, axis=1).astype(BF16)
    o_ref[0] = x + _dot(o, wo_ref[...])


def _xa(x3, kv, gain, wq, wo, tm):
    Bn, S, D = x3.shape
    M = kv.shape[1]
    tok = lambda b, i: (b, i, 0)
    return pl.pallas_call(
        _xa_kernel,
        grid=(Bn, S // tm),
        in_specs=[pl.BlockSpec((1, tm, D), tok), pl.BlockSpec((1, M, 2 * D), lambda b, i: (b, 0, 0)),
                  _const_spec(gain.shape), _const_spec(wq.shape), _const_spec(wo.shape)],
        out_specs=pl.BlockSpec((1, tm, D), tok),
        out_shape=jax.ShapeDtypeStruct((Bn, S, D), F32),
        compiler_params=_cparams(("parallel", "parallel")),
        name="xattn",
    )(x3, kv, gain, wq, wo)


def _ffn_kernel(x_ref, xp_ref, xn_ref, g_ref, wu_ref, cw_ref, cb_ref, wd_ref, o_ref, *, tf):
    i = pl.program_id(1)
    last = pl.num_programs(1) - 1
    tm = x_ref.shape[1]
    hal = xp_ref.shape[1]
    F = wd_ref.shape[0]
    x = x_ref[0]
    g = g_ref[...]
    hp = _rms(xp_ref[0], g) * (i > 0).astype(F32)
    hn = _rms(xn_ref[0], g) * (i < last).astype(F32)
    he = jnp.concatenate([hp, _rms(x, g), hn], axis=0).astype(BF16)
    ext = tm + 2 * hal

    def conv(u, cols):
        w = cw_ref[:, cols]
        m1 = pltpu.roll(u, 1, 0)[hal:hal + tm]
        p1 = pltpu.roll(u, ext - 1, 0)[hal:hal + tm]
        return w[0:1] * m1 + w[1:2] * u[hal:hal + tm] + w[2:3] * p1 + cb_ref[:, cols]

    acc = x
    for f in range(F // tf):
        gc = slice(f * tf, (f + 1) * tf)
        uc = slice(F + f * tf, F + (f + 1) * tf)
        gate = conv(_dot(he, wu_ref[:, gc]), gc)
        up = conv(_dot(he, wu_ref[:, uc]), uc)
        act = (gate * jax.nn.sigmoid(gate) * up).astype(BF16)
        acc = acc + _dot(act, wd_ref[gc, :])
    o_ref[0] = acc


def _ffn(x3, gain, w_up, conv_w, conv_b, w_down, tm, tf):
    Bn, S, D = x3.shape
    hal = 8
    nb = tm // hal
    tok = lambda b, i: (b, i, 0)
    resident = lambda shape: pl.BlockSpec(shape, lambda b, i: (0, 0), pipeline_mode=pl.Buffered(1))
    return pl.pallas_call(
        functools.partial(_ffn_kernel, tf=tf),
        grid=(Bn, S // tm),
        in_specs=[
            pl.BlockSpec((1, tm, D), tok),
            pl.BlockSpec((1, hal, D), lambda b, i: (b, jnp.maximum(i * nb - 1, 0), 0)),
            pl.BlockSpec((1, hal, D), lambda b, i: (b, jnp.minimum((i + 1) * nb, S // hal - 1), 0)),
            _const_spec(gain.shape), resident(w_up.shape), _const_spec(conv_w.shape), _const_spec(conv_b.shape),
            resident(w_down.shape),
        ],
        out_specs=pl.BlockSpec((1, tm, D), tok),
        out_shape=jax.ShapeDtypeStruct((Bn, S, D), F32),
        compiler_params=_cparams(("parallel", "parallel")),
        name="ffn",
    )(x3, x3, x3, gain, w_up, conv_w, conv_b, w_down)


def _norm_kernel(x_ref, g_ref, o_ref):
    o_ref[...] = _rms(x_ref[...], g_ref[...])


def _final_norm(x2, gain, tm):
    T, D = x2.shape
    return pl.pallas_call(
        _norm_kernel,
        grid=(T // tm,),
        in_specs=[pl.BlockSpec((tm, D), lambda i: (i, 0)), _const_spec(gain.shape)],
        out_specs=pl.BlockSpec((tm, D), lambda i: (i, 0)),
        out_shape=jax.ShapeDtypeStruct((T, D), F32),
        compiler_params=_cparams(("parallel",)),
        name="final_norm",
    )(x2, gain)


def _s5_weights(lam_re, lam_im, log_step, b_re, b_im, c_re, c_im, nb):
    G, P, J = b_re.shape[1:]
    gq = LANES // J
    nq = G // gq
    lr = jnp.minimum(lam_re, -1e-4)
    li = lam_im
    step = jnp.exp(log_step)[..., None]
    mag = jnp.exp(lr * step)
    a_re, a_im = mag * jnp.cos(li * step), mag * jnp.sin(li * step)
    den = lr * lr + li * li
    f_re = ((a_re - 1.0) * lr + a_im * li) / den
    f_im = (a_im * lr - (a_re - 1.0) * li) / den
    bb_re = f_re[..., None] * b_re - f_im[..., None] * b_im
    bb_im = f_re[..., None] * b_im + f_im[..., None] * b_re
    eye = jnp.eye(gq, dtype=F32)

    def expand(bb):
        t = bb.reshape(2, nq, gq, P, J)
        return jnp.einsum("dqgpj,gh->dqgjhp", t, eye).reshape(2, nq, gq * J, gq * P)

    def contract(cc):
        t = cc.reshape(2, nq, gq, J, P)
        return jnp.einsum("dqgjp,gh->dqgphj", t, eye).reshape(2, nq, gq * P, gq * J)

    bw = jnp.concatenate([expand(bb_re), expand(bb_im)], axis=3).astype(BF16)
    cw = jnp.concatenate([contract(c_re), -contract(c_im)], axis=2).astype(BF16)
    a_b = jnp.stack([a_re.reshape(2, G * P), a_im.reshape(2, G * P)], axis=1)
    a_b = jnp.broadcast_to(a_b[:, :, None, :], (2, 2, nb, G * P))
    return bw, cw, a_b


def _lru_gate_weights(wa, wx, ba, bx):
    nblk, bs = wa.shape[1], wa.shape[2]
    eye = jnp.eye(nblk, dtype=F32)
    dense = lambda w: jnp.einsum("dnkm,nl->dnklm", w, eye).reshape(2, nblk * bs, nblk * bs)
    wg = jnp.concatenate([dense(wa), dense(wx)], axis=2).astype(BF16)
    bg = jnp.concatenate([ba, bx], axis=1)[:, None, :]
    return wg, bg


def _pick(n, pref):
    return pref if n % pref == 0 else n


def kernel(x, mem, mix_norm, w_in, gdn_conv, gdn_a_log, gdn_dt_bias, gdn_out_norm, s5_lambda_re, s5_lambda_im, s5_log_step, s5_b_re, s5_b_im, s5_c_re, s5_c_im, s5_d, s5_glu_w, s5_glu_b, lru_conv_w, lru_conv_b, lru_gate_a_w, lru_gate_a_b, lru_gate_x_w, lru_gate_x_b, lru_lambda, sc_conv, w_branch, w_mix_out, xa_norm, xa_mem_norm, xa_w_q, xa_w_kv, xa_w_o, ffn_norm, ffn_w_up, ffn_conv_w, ffn_conv_b, ffn_w_down, final_norm):
    Bn, S, D = x.shape
    depth = w_in.shape[0]
    W = D // 2
    H = GDN_HEADS
    T = Bn * S
    F = ffn_w_down.shape[1]
    assert Bn % BF16_ROWS == 0 and S % 256 == 0 and W == H * LANES

    blk = {"z": 3, "s5_u": 4, "lru_x": 5, "lru_g": 6, "sc_b": 7, "sc_c": 8, "sc_x": 9, "gates": 10}
    n_ab = 4 * H
    c_ab = 4 * W
    tm_proj = _pick(T, 1024)
    ts_tok = _pick(S, 512)
    rows_tm = 32 * Bn

    row2 = lambda v: v.reshape(1, -1)
    for l in range(depth):
        w_main = jnp.concatenate([w_in[l][:, :c_ab], w_in[l][:, c_ab + n_ab:]], axis=1).astype(BF16)
        w_ab = jnp.pad(w_in[l][:, c_ab:c_ab + n_ab], ((0, 0), (0, LANES - n_ab))).astype(BF16)
        P, ab = _proj(x.reshape(T, D), row2(mix_norm[l]), w_main, w_ab, tm_proj, 1024)
        P3 = P.reshape(Bn, S, -1)

        qkv = _qkvprep(P3, gdn_conv[l], ts_tok)
        prm = jnp.zeros((8, LANES), F32)
        prm = prm.at[0, 2 * H:4 * H].set(gdn_a_log[l].reshape(-1)).at[1, 2 * H:4 * H].set(gdn_dt_bias[l].reshape(-1))
        o_f, o_b = _gdn(qkv, ab.reshape(Bn, S, LANES), prm, 256)

        utm = jnp.swapaxes(P3[:, :, blk["s5_u"] * W:(blk["lru_g"] + 1) * W], 0, 1).reshape(S * Bn, 3 * W)
        bw, cw, a_b = _s5_weights(s5_lambda_re[l], s5_lambda_im[l], s5_log_step[l], s5_b_re[l], s5_b_im[l],
                                  s5_c_re[l], s5_c_im[l], Bn)
        y_f, y_b = _s5(utm, bw, cw, a_b, Bn, rows_tm, 0)
        wg, bg = _lru_gate_weights(lru_gate_a_w[l], lru_gate_x_w[l], lru_gate_a_b[l], lru_gate_x_b[l])
        h_f, h_b = _lru(utm, lru_conv_w[l], row2(lru_conv_b[l]), wg, bg, lru_lambda[l][:, None, :], Bn, rows_tm, 1)
        sl = _post(utm, y_f, y_b, h_f, h_b, row2(s5_d[l]), s5_glu_w[l].astype(BF16), row2(s5_glu_b[l]),
                   rows_tm, 0, 2)
        sl3 = jnp.swapaxes(sl.reshape(S, Bn, 2 * W), 0, 1)

        x = _merge(x, o_f, o_b, P3, sl3, row2(gdn_out_norm[l]), sc_conv[l], w_branch[l].astype(BF16),
                   w_mix_out[l].astype(BF16), ts_tok, blk)

        kv = _kv(mem, row2(xa_mem_norm[l]), xa_w_kv[l].astype(BF16))
        x = _xa(x, kv, row2(xa_norm[l]), xa_w_q[l].astype(BF16), xa_w_o[l].astype(BF16), ts_tok)

        x = _ffn(x, row2(ffn_norm[l]), ffn_w_up[l].astype(BF16), ffn_conv_w[l], row2(ffn_conv_b[l]),
                 ffn_w_down[l].astype(BF16), ts_tok, 256)
    return _final_norm(x.reshape(T, D), row2(final_norm), tm_proj).reshape(Bn, S, D)
```

```python
import functools
import math

import jax
import jax.numpy as jnp
from jax import lax
from jax.experimental import pallas as pl
from jax.experimental.pallas import tpu as pltpu

F32 = jnp.float32
BF16 = jnp.bfloat16
EPS = 1e-6

LANES = 128
BF16_ROWS = 16
VMEM_LIMIT = 56 * 1024 * 1024

GDN_HEADS = 4
GDN_CHUNK = 64
GDN_CONV = 4
GDN_INV_PASSES = 1
LRU_CONV = 4
S5_GROUP = 16
S5_STATE = 64
LRU_C = 8.0
XA_HEADS = 4

HI = lax.Precision.HIGHEST


def _cparams(sem):
    return pltpu.CompilerParams(dimension_semantics=sem, vmem_limit_bytes=VMEM_LIMIT)


def _rms(x, g):
    return x * lax.rsqrt(jnp.mean(x * x, axis=-1, keepdims=True) + EPS) * g


def _dot(a, b):
    return jnp.dot(a, b, preferred_element_type=F32)


def _dot_nt(a, b):
    return lax.dot_general(a, b, (((1,), (1,)), ((), ())), preferred_element_type=F32)


def _sigmoid(x):
    return 0.5 * jnp.tanh(0.5 * x) + 0.5


def _softplus(x):
    return jnp.maximum(x, 0.0) + jnp.log(1.0 + jnp.exp(-jnp.abs(x)))


def _const_spec(shape):
    nd = len(shape)
    return pl.BlockSpec(shape, lambda *_: (0,) * nd)


def _proj_kernel(x_ref, g_ref, w_ref, wab_ref, p_ref, ab_ref, h_ref):
    @pl.when(pl.program_id(1) == 0)
    def _():
        hb = _rms(x_ref[...], g_ref[...]).astype(BF16)
        h_ref[...] = hb
        ab_ref[...] = _dot(hb, wab_ref[...])

    p_ref[...] = _dot(h_ref[...], w_ref[...]).astype(BF16)


def _proj(x2, gain, w_main, w_ab, tm, tn):
    T, D = x2.shape
    N = w_main.shape[1]
    return pl.pallas_call(
        _proj_kernel,
        grid=(T // tm, N // tn),
        in_specs=[
            pl.BlockSpec((tm, D), lambda i, j: (i, 0)),
            pl.BlockSpec((1, D), lambda i, j: (0, 0)),
            pl.BlockSpec((D, tn), lambda i, j: (0, j)),
            pl.BlockSpec((D, LANES), lambda i, j: (0, 0)),
        ],
        out_specs=[
            pl.BlockSpec((tm, tn), lambda i, j: (i, j)),
            pl.BlockSpec((tm, LANES), lambda i, j: (i, 0)),
        ],
        out_shape=[jax.ShapeDtypeStruct((T, N), BF16), jax.ShapeDtypeStruct((T, LANES), F32)],
        scratch_shapes=[pltpu.VMEM((tm, D), BF16)],
        compiler_params=_cparams(("parallel", "arbitrary")),
        name="proj",
    )(x2, gain, w_main, w_ab)


def _qkvprep_kernel(xm_ref, xp_ref, xn_ref, cw_ref, o_ref, *, n_norm):
    i = pl.program_id(1)
    last = pl.num_programs(1) - 1
    ts = xm_ref.shape[1]
    hal = xp_ref.shape[1]
    keep_p = (i > 0).astype(F32)
    keep_n = (i < last).astype(F32)
    row = lax.broadcasted_iota(jnp.int32, (ts, 1), 0)
    for c in range(xm_ref.shape[2] // LANES):
        sl = slice(c * LANES, (c + 1) * LANES)
        x = xm_ref[0, :, sl].astype(F32)
        xp = xp_ref[0, :, sl].astype(F32) * keep_p
        xn = xn_ref[0, :, sl].astype(F32) * keep_n
        m1 = jnp.where(row == 0, xp[hal - 1:hal], pltpu.roll(x, 1, 0))
        m2 = pltpu.roll(x, 2, 0)
        m2 = jnp.where(row == 0, xp[hal - 2:hal - 1], m2)
        m2 = jnp.where(row == 1, xp[hal - 1:hal], m2)
        p1 = jnp.where(row == ts - 1, xn[0:1], pltpu.roll(x, ts - 1, 0))
        w = cw_ref[:, sl]
        y = w[0:1] * m2 + w[1:2] * m1 + w[2:3] * x + w[3:4] * p1
        y = y * _sigmoid(y)
        if c < n_norm:
            y = y * lax.rsqrt(jnp.sum(y * y, axis=-1, keepdims=True) + EPS)
        o_ref[0, :, sl] = y.astype(o_ref.dtype)


def _qkvprep(P3, conv_w, ts):
    Bn, S, _ = P3.shape
    C = conv_w.shape[1]
    hal = BF16_ROWS
    nb = ts // hal
    return pl.pallas_call(
        functools.partial(_qkvprep_kernel, n_norm=2 * GDN_HEADS),
        grid=(Bn, S // ts),
        in_specs=[
            pl.BlockSpec((1, ts, C), lambda b, i: (b, i, 0)),
            pl.BlockSpec((1, hal, C), lambda b, i: (b, jnp.maximum(i * nb - 1, 0), 0)),
            pl.BlockSpec((1, hal, C), lambda b, i: (b, jnp.minimum((i + 1) * nb, S // hal - 1), 0)),
            _const_spec(conv_w.shape),
        ],
        out_specs=pl.BlockSpec((1, ts, C), lambda b, i: (b, i, 0)),
        out_shape=jax.ShapeDtypeStruct((Bn, S, C), BF16),
        compiler_params=_cparams(("parallel", "parallel")),
        name="gdn_prep",
    )(P3, P3, P3, conv_w)


def _mm_split(lhs, w, n_pass):
    lh, wh = lhs.astype(BF16), w.astype(BF16)
    if n_pass == 1:
        return _dot(lh, wh)
    ll = (lhs - lh.astype(F32)).astype(BF16)
    wl = (w - wh.astype(F32)).astype(BF16)
    return _dot(lh, wh) + (_dot(lh, wl) + _dot(ll, wh))


def _gdn_prepare(d, qc, gam, beta, grow, msk):
    C = GDN_CHUNK
    H = GDN_HEADS
    Dh = LANES
    blk, incl, strict, eye = msk
    stack = lambda off: jnp.concatenate([qc[:, off + h * Dh: off + (h + 1) * Dh] for h in range(H)], axis=0)
    Qs = stack(0).astype(F32) * (Dh ** -0.5)
    Kb16 = stack(H * Dh)
    Ks = Kb16.astype(F32)
    Vs = stack(2 * H * Dh).astype(F32)
    gcol = jnp.concatenate(gam, axis=0)
    bcol = jnp.concatenate(beta, axis=0)
    edge = C - 1 if d == 0 else 0
    glast = [g[edge:edge + 1] for g in gam]
    glast_col = jnp.concatenate([jnp.broadcast_to(g, (C, 1)) for g in glast], axis=0)

    diff = _side_by_side(gam, blk) - grow
    decay = jnp.where(incl, jnp.exp(jnp.where(incl, diff, 0.0)), 0.0)
    kb = Ks * bcol
    g2 = _dot_nt(jnp.concatenate([kb, Qs], axis=0).astype(BF16), Kb16)
    kk = _side_by_side([g2[h * C:(h + 1) * C] for h in range(H)], blk)
    qk = _side_by_side([g2[(H + h) * C:(H + h + 1) * C] for h in range(H)], blk) * decay
    X = jnp.where(strict, -(kk * decay), 0.0)
    egam = jnp.exp(gcol)
    return dict(X=X, P=eye + X, qk=qk, rhs=jnp.concatenate([Vs * bcol, kb * egam], axis=1).astype(BF16),
                qdec=Qs * egam, kdec=Ks * jnp.exp(glast_col - gcol), eg=[jnp.exp(g) for g in glast])


def _side_by_side(cols, blk):
    out = jnp.where(blk[0], cols[0], 0.0)
    for h in range(1, len(blk)):
        out = jnp.where(blk[h], cols[h], out)
    return out


def _bd(m, blk):
    return jnp.concatenate([jnp.where(b, m, 0.0) for b in blk], axis=0)


def _gdn_inverse_step(c, k, blk):
    C = GDN_CHUNK
    w = _bd(c["X"], blk)
    if k == 0:
        c["X"] = _mm_split(c["X"], w, GDN_INV_PASSES)
    elif k < 5:
        r = _mm_split(jnp.concatenate([c["P"], c["X"]], axis=0), w, GDN_INV_PASSES)
        c["P"] = c["P"] + r[:C]
        c["X"] = r[C:]
    else:
        c["P"] = c["P"] + _mm_split(c["P"], w, GDN_INV_PASSES)


def _gdn_wy(c, blk, kblk):
    C = GDN_CHUNK
    H = GDN_HEADS
    wy = _dot(_bd(c["P"], blk).astype(BF16), c["rhs"])
    c["U"], Wm = wy[:, :LANES], wy[:, LANES:]
    qdec = c["qdec"]
    c["lhs"] = [jnp.concatenate([Wm[h * C:(h + 1) * C], qdec[h * C:(h + 1) * C]], axis=0).astype(BF16)
                for h in range(H)]
    kdecT = c["kdec"].T
    c["kT"] = [jnp.where(kblk == h, kdecT, 0.0).astype(BF16) for h in range(H)]
    c["qkbd"] = _bd(c["qk"], blk).astype(BF16)


def _gdn_kernel(qf_ref, af_ref, qb_ref, abk_ref, prm_ref, of_ref, ob_ref, sf_ref, sb_ref):
    C = GDN_CHUNK
    H = GDN_HEADS
    CT = qf_ref.shape[1]
    nch = CT // C

    @pl.when(pl.program_id(1) == 0)
    def _():
        sf_ref[...] = jnp.zeros_like(sf_ref)
        sb_ref[...] = jnp.zeros_like(sb_ref)

    ri = lax.broadcasted_iota(jnp.int32, (C, H * C), 0)
    lane = lax.broadcasted_iota(jnp.int32, (C, H * C), 1)
    cj = lane % C
    blk = [(lane // C) == h for h in range(H)]
    eye = (cj == ri).astype(F32)
    kblk = lax.broadcasted_iota(jnp.int32, (LANES, H * C), 1) // C
    rt = lax.broadcasted_iota(jnp.int32, (CT, CT), 0)
    ct = lax.broadcasted_iota(jnp.int32, (CT, CT), 1)
    same_t = (rt // C) == (ct // C)
    neg_a = -jnp.exp(prm_ref[0:1, :])
    dt_b = prm_ref[1:2, :]

    dirs = ((0, qf_ref, af_ref, of_ref, sf_ref), (1, qb_ref, abk_ref, ob_ref, sb_ref))

    chunks = {}
    for d, q_ref, a_ref, o_ref, s_ref in dirs:
        ab = a_ref[0]
        bt = _sigmoid(ab)
        la = neg_a * _softplus(ab + dt_b)
        cum = (same_t & ((ct <= rt) if d == 0 else (ct >= rt))).astype(F32)
        gam = jnp.dot(cum, la, precision=HI, preferred_element_type=F32)
        slab = gam.T[2 * H:4 * H, :]
        msk = (blk, (cj <= ri) if d == 0 else (cj >= ri), (cj < ri) if d == 0 else (cj > ri), eye)
        for ci in range(nch):
            rows = slice(ci * C, (ci + 1) * C)
            gl = [gam[rows, 2 * H + d * H + h: 2 * H + d * H + h + 1] for h in range(H)]
            bl = [bt[rows, d * H + h: d * H + h + 1] for h in range(H)]
            pieces = []
            for h in range(H):
                shift = ((h - ci) * C) % CT
                moved = slab if shift == 0 else pltpu.roll(slab, shift, 1)
                pieces.append(moved[d * H + h:d * H + h + 1, :])
            grow = _side_by_side(pieces, [b[0:1, :] for b in blk])
            chunks[d, ci] = _gdn_prepare(d, q_ref[0, rows, :], gl, bl, grow, msk)

    order = [(d, step if d == 0 else nch - 1 - step) for step in range(nch) for d in (0, 1)]
    for k in range(6):
        for key in order:
            _gdn_inverse_step(chunks[key], k, blk)
    for key in order:
        _gdn_wy(chunks[key], blk, kblk)

    state = {d: [s_ref[h] for h in range(H)] for d, _, _, _, s_ref in dirs}
    for step in range(nch):
        cur = {d: chunks[d, step if d == 0 else nch - 1 - step] for d in (0, 1)}
        r = {d: [_dot(cur[d]["lhs"][h], state[d][h].astype(BF16)) for h in range(H)] for d in (0, 1)}
        vb = {}
        for d, _, _, o_ref, _ in dirs:
            c = cur[d]
            vnew = jnp.concatenate([c["U"][h * C:(h + 1) * C] - r[d][h][:C] for h in range(H)], axis=0)
            vb[d] = vnew.astype(BF16)
            out = jnp.concatenate([r[d][h][C:] for h in range(H)], axis=0) + _dot(c["qkbd"], vb[d])
            ci = step if d == 0 else nch - 1 - step
            for h in range(H):
                o_ref[0, ci * C:(ci + 1) * C, h * LANES:(h + 1) * LANES] = out[h * C:(h + 1) * C]
        for d in (0, 1):
            for h in range(H):
                state[d][h] = state[d][h] * cur[d]["eg"][h] + _dot(cur[d]["kT"][h], vb[d])
    for d, _, _, _, s_ref in dirs:
        for h in range(H):
            s_ref[h] = state[d][h]


def _gdn(qkv, ab3, prm, ct):
    Bn, S, C3 = qkv.shape
    W = C3 // 3
    nt = S // ct
    dh = W // GDN_HEADS
    assert ct == GDN_HEADS * GDN_CHUNK and dh == LANES
    fwd = lambda b, n: (b, n, 0)
    bwd = lambda b, n: (b, nt - 1 - n, 0)
    return pl.pallas_call(
        _gdn_kernel,
        grid=(Bn, nt),
        in_specs=[
            pl.BlockSpec((1, ct, C3), fwd),
            pl.BlockSpec((1, ct, LANES), fwd),
            pl.BlockSpec((1, ct, C3), bwd),
            pl.BlockSpec((1, ct, LANES), bwd),
            _const_spec(prm.shape),
        ],
        out_specs=[pl.BlockSpec((1, ct, W), fwd), pl.BlockSpec((1, ct, W), bwd)],
        out_shape=[jax.ShapeDtypeStruct((Bn, S, W), F32)] * 2,
        scratch_shapes=[pltpu.VMEM((GDN_HEADS, dh, dh), F32)] * 2,
        compiler_params=_cparams(("parallel", "arbitrary")),
        name="gdn",
    )(qkv, ab3, qkv, ab3, prm)


def _s5_kernel(uf_ref, ub_ref, bw_ref, cw_ref, a_ref, yf_ref, yb_ref, hs_ref, carry_ref, *, nb):
    R = uf_ref.shape[0]
    tt = R // nb
    nq = bw_ref.shape[1]
    half = bw_ref.shape[3] // 2

    @pl.when(pl.program_id(0) == 0)
    def _():
        carry_ref[...] = jnp.zeros_like(carry_ref)

    for d, (u_ref, y_ref) in enumerate(((uf_ref, yf_ref), (ub_ref, yb_ref))):
        for q in range(nq):
            hs_ref[d, :, 2 * half * q:2 * half * (q + 1)] = _dot(u_ref[:, q * LANES:(q + 1) * LANES], bw_ref[d, q])
        for q in range(nq):
            re = slice(2 * half * q, 2 * half * q + half)
            im = slice(2 * half * q + half, 2 * half * (q + 1))
            ar = a_ref[d, 0, :, q * half:(q + 1) * half]
            ai = a_ref[d, 1, :, q * half:(q + 1) * half]

            def body(s, carry, d=d, re=re, im=im, ar=ar, ai=ai):
                hr, hi = carry
                t = s if d == 0 else tt - 1 - s
                rows = pl.ds(pl.multiple_of(t * nb, nb), nb)
                nr = ar * hr - ai * hi + hs_ref[d, rows, re]
                ni = ar * hi + ai * hr + hs_ref[d, rows, im]
                hs_ref[d, rows, re] = nr
                hs_ref[d, rows, im] = ni
                return nr, ni

            hr, hi = lax.fori_loop(0, tt, body, (carry_ref[d, :, re], carry_ref[d, :, im]), unroll=4)
            carry_ref[d, :, re] = hr
            carry_ref[d, :, im] = hi
        for q in range(nq):
            hq = hs_ref[d, :, 2 * half * q:2 * half * (q + 1)].astype(BF16)
            y_ref[:, q * LANES:(q + 1) * LANES] = _dot(hq, cw_ref[d, q])


def _s5(utm, bw, cw, a_b, nb, rows, col_blk):
    TB = utm.shape[0]
    nq = bw.shape[1]
    W = nq * LANES
    nstate = bw.shape[3] * nq
    nt = TB // rows
    fwd = lambda n: (n, col_blk)
    bwd = lambda n: (nt - 1 - n, col_blk)
    return pl.pallas_call(
        functools.partial(_s5_kernel, nb=nb),
        grid=(nt,),
        in_specs=[
            pl.BlockSpec((rows, W), fwd),
            pl.BlockSpec((rows, W), bwd),
            _const_spec(bw.shape),
            _const_spec(cw.shape),
            _const_spec(a_b.shape),
        ],
        out_specs=[pl.BlockSpec((rows, W), lambda n: (n, 0)), pl.BlockSpec((rows, W), lambda n: (nt - 1 - n, 0))],
        out_shape=[jax.ShapeDtypeStruct((TB, W), F32)] * 2,
        scratch_shapes=[pltpu.VMEM((2, rows, nstate), F32), pltpu.VMEM((2, nb, nstate), F32)],
        compiler_params=_cparams(("arbitrary",)),
        name="s5_scan",
    )(utm, utm, bw, cw, a_b)


def _lru_kernel(xf_ref, xfp_ref, xfn_ref, xb_ref, xbp_ref, xbn_ref, cw_ref, cb_ref, wg_ref, bg_ref, lam_ref,
                of_ref, ob_ref, a_s, b_s, carry_ref, *, nb):
    n = pl.program_id(0)
    nt = pl.num_programs(0)
    R = xf_ref.shape[0]
    W = xf_ref.shape[1]
    tt = R // nb

    @pl.when(n == 0)
    def _():
        carry_ref[...] = jnp.zeros_like(carry_ref)

    streams = ((0, n, xf_ref, xfp_ref, xfn_ref, of_ref), (1, nt - 1 - n, xb_ref, xbp_ref, xbn_ref, ob_ref))
    for d, pos, xm_ref, xp_ref, xn_ref, o_ref in streams:
        xp = xp_ref[...].astype(F32) * (pos > 0).astype(F32)
        xn = xn_ref[...].astype(F32) * (pos < nt - 1).astype(F32)
        xe = jnp.concatenate([xp, xm_ref[...].astype(F32), xn], axis=0)
        w = cw_ref[...]
        xc = cb_ref[...] + sum(w[k:k + 1] * xe[k * nb:k * nb + R] for k in range(LRU_CONV))
        gates = _dot(xc.astype(BF16), wg_ref[d]) + bg_ref[d]
        rg = _sigmoid(gates[:, :W])
        ig = _sigmoid(gates[:, W:])
        a = jnp.exp(-LRU_C * rg * _softplus(-lam_ref[d]))
        a_s[d] = a
        b_s[d] = jnp.sqrt(1.0 - a * a) * (ig * xc)

        def body(s, h, d=d, o_ref=o_ref):
            t = s if d == 0 else tt - 1 - s
            rows = pl.ds(pl.multiple_of(t * nb, nb), nb)
            h = a_s[d, rows, :] * h + b_s[d, rows, :]
            o_ref[rows, :] = h
            return h

        carry_ref[d] = lax.fori_loop(0, tt, body, carry_ref[d], unroll=4)


def _lru(utm, conv_w, conv_b, wg, bg, lam, nb, rows, col_blk):
    TB = utm.shape[0]
    W = conv_w.shape[1]
    nt = TB // rows
    hp = 2 * nb
    hn = nb
    nbt = TB // nb

    def specs(pos):
        return [
            pl.BlockSpec((rows, W), lambda n: (pos(n), col_blk)),
            pl.BlockSpec((hp, W), lambda n: (jnp.maximum(pos(n) * (rows // hp) - 1, 0), col_blk)),
            pl.BlockSpec((hn, W), lambda n: (jnp.minimum((pos(n) + 1) * (rows // hn), nbt - 1), col_blk)),
        ]

    fwd = lambda n: n
    bwd = lambda n: nt - 1 - n
    return pl.pallas_call(
        functools.partial(_lru_kernel, nb=nb),
        grid=(nt,),
        in_specs=specs(fwd) + specs(bwd) + [_const_spec(conv_w.shape), _const_spec(conv_b.shape),
                                            _const_spec(wg.shape), _const_spec(bg.shape), _const_spec(lam.shape)],
        out_specs=[pl.BlockSpec((rows, W), lambda n: (n, 0)), pl.BlockSpec((rows, W), lambda n: (nt - 1 - n, 0))],
        out_shape=[jax.ShapeDtypeStruct((TB, W), F32)] * 2,
        scratch_shapes=[pltpu.VMEM((2, rows, W), F32), pltpu.VMEM((2, rows, W), F32), pltpu.VMEM((2, nb, W), F32)],
        compiler_params=_cparams(("arbitrary",)),
        name="lru_scan",
    )(utm, utm, utm, utm, utm, utm, conv_w, conv_b, wg, bg, lam)


def _post_kernel(u_ref, g_ref, yf_ref, yb_ref, hf_ref, hb_ref, d_ref, gw_ref, gb_ref, o_ref):
    W = u_ref.shape[1]
    y = u_ref[...].astype(F32) * d_ref[...] + yf_ref[...] + yb_ref[...]
    zg = jax.nn.gelu(y)
    o_ref[:, :W] = (zg * _sigmoid(_dot(zg.astype(BF16), gw_ref[...]) + gb_ref[...])).astype(o_ref.dtype)
    o_ref[:, W:] = ((hf_ref[...] + hb_ref[...]) * jax.nn.gelu(g_ref[...].astype(F32))).astype(o_ref.dtype)


def _post(utm, yf, yb, hf, hb, s5_d, glu_w, glu_b, rows, u_blk, g_blk):
    TB, W = yf.shape
    row = lambda n: (n, 0)
    return pl.pallas_call(
        _post_kernel,
        grid=(TB // rows,),
        in_specs=[
            pl.BlockSpec((rows, W), lambda n: (n, u_blk)),
            pl.BlockSpec((rows, W), lambda n: (n, g_blk)),
            pl.BlockSpec((rows, W), row), pl.BlockSpec((rows, W), row),
            pl.BlockSpec((rows, W), row), pl.BlockSpec((rows, W), row),
            _const_spec(s5_d.shape), _const_spec(glu_w.shape), _const_spec(glu_b.shape),
        ],
        out_specs=pl.BlockSpec((rows, 2 * W), row),
        out_shape=jax.ShapeDtypeStruct((TB, 2 * W), BF16),
        compiler_params=_cparams(("parallel",)),
        name="s5_lru_post",
    )(utm, utm, yf, yb, hf, hb, s5_d, glu_w, glu_b)


def _merge_kernel(x_ref, of_ref, ob_ref, z_ref, sl_ref, scb_ref, scc_ref, scx_ref, ccp_ref, cxp_ref, ccn_ref,
                  cxn_ref, g0_ref, g1_ref, g2_ref, g3_ref, gn_ref, scw_ref, wb_ref, wo_ref, o_ref):
    i = pl.program_id(1)
    last = pl.num_programs(1) - 1
    tm = x_ref.shape[1]
    W = z_ref.shape[2]
    hal = ccp_ref.shape[1]

    o = of_ref[0] + ob_ref[0]
    z = z_ref[0].astype(F32)
    ys = []
    for h in range(GDN_HEADS):
        oh = o[:, h * LANES:(h + 1) * LANES]
        ys.append(oh * lax.rsqrt(jnp.mean(oh * oh, axis=-1, keepdims=True) + EPS) * gn_ref[...])
    y_gdn = jnp.concatenate(ys, axis=1) * (z * _sigmoid(z))

    cx = scc_ref[0].astype(F32) * scx_ref[0].astype(F32)
    cx_p = (ccp_ref[0, hal - 1:hal].astype(F32) * cxp_ref[0, hal - 1:hal].astype(F32)) * (i > 0).astype(F32)
    cx_n = (ccn_ref[0, 0:1].astype(F32) * cxn_ref[0, 0:1].astype(F32)) * (i < last).astype(F32)
    row = lax.broadcasted_iota(jnp.int32, (tm, 1), 0)
    m1 = jnp.where(row == 0, cx_p, pltpu.roll(cx, 1, 0))
    p1 = jnp.where(row == tm - 1, cx_n, pltpu.roll(cx, tm - 1, 0))
    w = scw_ref[...]
    y_sc = scb_ref[0].astype(F32) * (w[0:1] * m1 + w[1:2] * cx + w[2:3] * p1)

    branches = (y_gdn.astype(BF16), sl_ref[0, :, :W], sl_ref[0, :, W:], y_sc.astype(BF16))
    merged = None
    for m, (y, g_ref) in enumerate(zip(branches, (g0_ref, g1_ref, g2_ref, g3_ref))):
        t = _sigmoid(g_ref[0].astype(F32)) * _dot(y, wb_ref[m])
        merged = t if merged is None else merged + t
    o_ref[0] = x_ref[0] + _dot(merged.astype(BF16), wo_ref[...])


def _merge(x3, o_f, o_b, P3, sl3, gdn_gain, sc_w, w_branch, w_out, tm, blk):
    Bn, S, D = x3.shape
    W = o_f.shape[2]
    hal = BF16_ROWS
    nb = tm // hal
    tok = lambda b, i: (b, i, 0)
    col = lambda k: (lambda b, i: (b, i, k))
    prev = lambda k: (lambda b, i: (b, jnp.maximum(i * nb - 1, 0), k))
    nxt = lambda k: (lambda b, i: (b, jnp.minimum((i + 1) * nb, S // hal - 1), k))
    gate0 = blk["gates"] * W // D
    return pl.pallas_call(
        _merge_kernel,
        grid=(Bn, S // tm),
        in_specs=[
            pl.BlockSpec((1, tm, D), tok),
            pl.BlockSpec((1, tm, W), tok), pl.BlockSpec((1, tm, W), tok),
            pl.BlockSpec((1, tm, W), col(blk["z"])),
            pl.BlockSpec((1, tm, 2 * W), tok),
            pl.BlockSpec((1, tm, W), col(blk["sc_b"])),
            pl.BlockSpec((1, tm, W), col(blk["sc_c"])),
            pl.BlockSpec((1, tm, W), col(blk["sc_x"])),
            pl.BlockSpec((1, hal, W), prev(blk["sc_c"])), pl.BlockSpec((1, hal, W), prev(blk["sc_x"])),
            pl.BlockSpec((1, hal, W), nxt(blk["sc_c"])), pl.BlockSpec((1, hal, W), nxt(blk["sc_x"])),
            pl.BlockSpec((1, tm, D), col(gate0)), pl.BlockSpec((1, tm, D), col(gate0 + 1)),
            pl.BlockSpec((1, tm, D), col(gate0 + 2)), pl.BlockSpec((1, tm, D), col(gate0 + 3)),
            _const_spec(gdn_gain.shape), _const_spec(sc_w.shape), _const_spec(w_branch.shape),
            _const_spec(w_out.shape),
        ],
        out_specs=pl.BlockSpec((1, tm, D), tok),
        out_shape=jax.ShapeDtypeStruct((Bn, S, D), F32),
        compiler_params=_cparams(("parallel", "parallel")),
        name="merge",
    )(x3, o_f, o_b, P3, sl3, P3, P3, P3, P3, P3, P3, P3, P3, P3, P3, P3, gdn_gain, sc_w, w_branch, w_out)


def _kv_kernel(m_ref, g_ref, w_ref, o_ref):
    o_ref[0] = _dot(_rms(m_ref[0], g_ref[...]).astype(BF16), w_ref[...]).astype(o_ref.dtype)


def _kv(mem, gain, w_kv):
    Bn, M, D = mem.shape
    return pl.pallas_call(
        _kv_kernel,
        grid=(Bn,),
        in_specs=[pl.BlockSpec((1, M, D), lambda b: (b, 0, 0)), _const_spec(gain.shape), _const_spec(w_kv.shape)],
        out_specs=pl.BlockSpec((1, M, 2 * D), lambda b: (b, 0, 0)),
        out_shape=jax.ShapeDtypeStruct((Bn, M, 2 * D), BF16),
        compiler_params=_cparams(("parallel",)),
        name="xa_kv",
    )(mem, gain, w_kv)


def _xa_kernel(x_ref, kv_ref, g_ref, wq_ref, wo_ref, o_ref):
    D = x_ref.shape[2]
    dh = D // XA_HEADS
    x = x_ref[0]
    q = _dot(_rms(x, g_ref[...]).astype(BF16), wq_ref[...]).astype(BF16)
    outs = []
    for h in range(XA_HEADS):
        k = kv_ref[0, :, h * dh:(h + 1) * dh]
        v = kv_ref[0, :, D + h * dh:D + (h + 1) * dh]
        s = _dot_nt(q[:, h * dh:(h + 1) * dh], k) * (dh ** -0.5)
        e = jnp.exp(s - jnp.max(s, axis=-1, keepdims=True))
        p = e / jnp.sum(e, axis=-1, keepdims=True)
        outs.append(_dot(p.astype(BF16), v))
    o = jnp.concatenate(outs, axis=1).astype(BF16)
    o_ref[0] = x + _dot(o, wo_ref[...])


def _xa(x3, kv, gain, wq, wo, tm):
    Bn, S, D = x3.shape
    M = kv.shape[1]
    tok = lambda b, i: (b, i, 0)
    return pl.pallas_call(
        _xa_kernel,
        grid=(Bn, S // tm),
        in_specs=[pl.BlockSpec((1, tm, D), tok), pl.BlockSpec((1, M, 2 * D), lambda b, i: (b, 0, 0)),
                  _const_spec(gain.shape), _const_spec(wq.shape), _const_spec(wo.shape)],
        out_specs=pl.BlockSpec((1, tm, D), tok),
        out_shape=jax.ShapeDtypeStruct((Bn, S, D), F32),
        compiler_params=_cparams(("parallel", "parallel")),
        name="xattn",
    )(x3, kv, gain, wq, wo)


def _ffn_kernel(x_ref, xp_ref, xn_ref, g_ref, wu_ref, cw_ref, cb_ref, wd_ref, o_ref, act_ref, *, tf):
    i = pl.program_id(1)
    last = pl.num_programs(1) - 1
    tm = x_ref.shape[1]
    hal = xp_ref.shape[1]
    F = wd_ref.shape[0]
    x = x_ref[0]
    g = g_ref[...]
    hp = _rms(xp_ref[0], g) * (i > 0).astype(F32)
    hn = _rms(xn_ref[0], g) * (i < last).astype(F32)
    he = jnp.concatenate([hp, _rms(x, g), hn], axis=0).astype(BF16)
    ext = tm + 2 * hal

    def conv(u, cols):
        w = cw_ref[:, cols]
        m1 = pltpu.roll(u, 1, 0)[hal:hal + tm]
        p1 = pltpu.roll(u, ext - 1, 0)[hal:hal + tm]
        return w[0:1] * m1 + w[1:2] * u[hal:hal + tm] + w[2:3] * p1 + cb_ref[:, cols]

    for f in range(F // tf):
        gc = slice(f * tf, (f + 1) * tf)
        uc = slice(F + f * tf, F + (f + 1) * tf)
        gate = conv(_dot(he, wu_ref[:, gc]), gc)
        up = conv(_dot(he, wu_ref[:, uc]), uc)
        act_ref[:, gc] = (gate * _sigmoid(gate) * up).astype(BF16)
    o_ref[0] = x + _dot(act_ref[...], wd_ref[...])


def _ffn(x3, gain, w_up, conv_w, conv_b, w_down, tm, tf):
    Bn, S, D = x3.shape
    hal = 8
    nb = tm // hal
    tok = lambda b, i: (b, i, 0)
    resident = lambda shape: pl.BlockSpec(shape, lambda b, i: (0, 0), pipeline_mode=pl.Buffered(1))
    return pl.pallas_call(
        functools.partial(_ffn_kernel, tf=tf),
        grid=(Bn, S // tm),
        in_specs=[
            pl.BlockSpec((1, tm, D), tok),
            pl.BlockSpec((1, hal, D), lambda b, i: (b, jnp.maximum(i * nb - 1, 0), 0)),
            pl.BlockSpec((1, hal, D), lambda b, i: (b, jnp.minimum((i + 1) * nb, S // hal - 1), 0)),
            _const_spec(gain.shape), resident(w_up.shape), _const_spec(conv_w.shape), _const_spec(conv_b.shape),
            resident(w_down.shape),
        ],
        out_specs=pl.BlockSpec((1, tm, D), tok),
        out_shape=jax.ShapeDtypeStruct((Bn, S, D), F32),
        scratch_shapes=[pltpu.VMEM((tm, w_down.shape[0]), BF16)],
        compiler_params=_cparams(("parallel", "parallel")),
        name="ffn",
    )(x3, x3, x3, gain, w_up, conv_w, conv_b, w_down)


def _norm_kernel(x_ref, g_ref, o_ref):
    o_ref[...] = _rms(x_ref[...], g_ref[...])


def _final_norm(x2, gain, tm):
    T, D = x2.shape
    return pl.pallas_call(
        _norm_kernel,
        grid=(T // tm,),
        in_specs=[pl.BlockSpec((tm, D), lambda i: (i, 0)), _const_spec(gain.shape)],
        out_specs=pl.BlockSpec((tm, D), lambda i: (i, 0)),
        out_shape=jax.ShapeDtypeStruct((T, D), F32),
        compiler_params=_cparams(("parallel",)),
        name="final_norm",
    )(x2, gain)


def _s5_weights(lam_re, lam_im, log_step, b_re, b_im, c_re, c_im, nb):
    G, P, J = b_re.shape[1:]
    gq = LANES // J
    nq = G // gq
    lr = jnp.minimum(lam_re, -1e-4)
    li = lam_im
    step = jnp.exp(log_step)[..., None]
    mag = jnp.exp(lr * step)
    a_re, a_im = mag * jnp.cos(li * step), mag * jnp.sin(li * step)
    den = lr * lr + li * li
    f_re = ((a_re - 1.0) * lr + a_im * li) / den
    f_im = (a_im * lr - (a_re - 1.0) * li) / den
    bb_re = f_re[..., None] * b_re - f_im[..., None] * b_im
    bb_im = f_re[..., None] * b_im + f_im[..., None] * b_re
    eye = jnp.eye(gq, dtype=F32)

    def expand(bb):
        t = bb.reshape(2, nq, gq, P, J)
        return jnp.einsum("dqgpj,gh->dqgjhp", t, eye).reshape(2, nq, gq * J, gq * P)

    def contract(cc):
        t = cc.reshape(2, nq, gq, J, P)
        return jnp.einsum("dqgjp,gh->dqgphj", t, eye).reshape(2, nq, gq * P, gq * J)

    bw = jnp.concatenate([expand(bb_re), expand(bb_im)], axis=3).astype(BF16)
    cw = jnp.concatenate([contract(c_re), -contract(c_im)], axis=2).astype(BF16)
    a_b = jnp.stack([a_re.reshape(2, G * P), a_im.reshape(2, G * P)], axis=1)
    a_b = jnp.broadcast_to(a_b[:, :, None, :], (2, 2, nb, G * P))
    return bw, cw, a_b


def _lru_gate_weights(wa, wx, ba, bx):
    nblk, bs = wa.shape[1], wa.shape[2]
    eye = jnp.eye(nblk, dtype=F32)
    dense = lambda w: jnp.einsum("dnkm,nl->dnklm", w, eye).reshape(2, nblk * bs, nblk * bs)
    wg = jnp.concatenate([dense(wa), dense(wx)], axis=2).astype(BF16)
    bg = jnp.concatenate([ba, bx], axis=1)[:, None, :]
    return wg, bg


def _pick(n, pref):
    return pref if n % pref == 0 else n


def kernel(x, mem, mix_norm, w_in, gdn_conv, gdn_a_log, gdn_dt_bias, gdn_out_norm, s5_lambda_re, s5_lambda_im, s5_log_step, s5_b_re, s5_b_im, s5_c_re, s5_c_im, s5_d, s5_glu_w, s5_glu_b, lru_conv_w, lru_conv_b, lru_gate_a_w, lru_gate_a_b, lru_gate_x_w, lru_gate_x_b, lru_lambda, sc_conv, w_branch, w_mix_out, xa_norm, xa_mem_norm, xa_w_q, xa_w_kv, xa_w_o, ffn_norm, ffn_w_up, ffn_conv_w, ffn_conv_b, ffn_w_down, final_norm):
    Bn, S, D = x.shape
    depth = w_in.shape[0]
    W = D // 2
    H = GDN_HEADS
    T = Bn * S
    F = ffn_w_down.shape[1]
    assert Bn % BF16_ROWS == 0 and S % 256 == 0 and W == H * LANES

    blk = {"z": 3, "s5_u": 4, "lru_x": 5, "lru_g": 6, "sc_b": 7, "sc_c": 8, "sc_x": 9, "gates": 10}
    n_ab = 4 * H
    c_ab = 4 * W
    tm_proj = _pick(T, 1024)
    ts_tok = _pick(S, 512)
    rows_tm = 32 * Bn

    row2 = lambda v: v.reshape(1, -1)
    for l in range(depth):
        w_main = jnp.concatenate([w_in[l][:, :c_ab], w_in[l][:, c_ab + n_ab:]], axis=1).astype(BF16)
        w_ab = jnp.pad(w_in[l][:, c_ab:c_ab + n_ab], ((0, 0), (0, LANES - n_ab))).astype(BF16)
        P, ab = _proj(x.reshape(T, D), row2(mix_norm[l]), w_main, w_ab, tm_proj, _pick(w_main.shape[1], 2304))
        P3 = P.reshape(Bn, S, -1)

        qkv = _qkvprep(P3, gdn_conv[l], ts_tok)
        prm = jnp.zeros((8, LANES), F32)
        prm = prm.at[0, 2 * H:4 * H].set(gdn_a_log[l].reshape(-1)).at[1, 2 * H:4 * H].set(gdn_dt_bias[l].reshape(-1))
        o_f, o_b = _gdn(qkv, ab.reshape(Bn, S, LANES), prm, 256)

        utm = jnp.swapaxes(P3[:, :, blk["s5_u"] * W:(blk["lru_g"] + 1) * W], 0, 1).reshape(S * Bn, 3 * W)
        bw, cw, a_b = _s5_weights(s5_lambda_re[l], s5_lambda_im[l], s5_log_step[l], s5_b_re[l], s5_b_im[l],
                                  s5_c_re[l], s5_c_im[l], Bn)
        y_f, y_b = _s5(utm, bw, cw, a_b, Bn, rows_tm, 0)
        wg, bg = _lru_gate_weights(lru_gate_a_w[l], lru_gate_x_w[l], lru_gate_a_b[l], lru_gate_x_b[l])
        h_f, h_b = _lru(utm, lru_conv_w[l], row2(lru_conv_b[l]), wg, bg, lru_lambda[l][:, None, :], Bn, rows_tm, 1)
        sl = _post(utm, y_f, y_b, h_f, h_b, row2(s5_d[l]), s5_glu_w[l].astype(BF16), row2(s5_glu_b[l]),
                   rows_tm, 0, 2)
        sl3 = jnp.swapaxes(sl.reshape(S, Bn, 2 * W), 0, 1)

        x = _merge(x, o_f, o_b, P3, sl3, row2(gdn_out_norm[l]), sc_conv[l], w_branch[l].astype(BF16),
                   w_mix_out[l].astype(BF16), ts_tok, blk)

        kv = _kv(mem, row2(xa_mem_norm[l]), xa_w_kv[l].astype(BF16))
        x = _xa(x, kv, row2(xa_norm[l]), xa_w_q[l].astype(BF16), xa_w_o[l].astype(BF16), ts_tok)

        x = _ffn(x, row2(ffn_norm[l]), ffn_w_up[l].astype(BF16), ffn_conv_w[l], row2(ffn_conv_b[l]),
                 ffn_w_down[l].astype(BF16), ts_tok, 256)
    return _final_norm(x.reshape(T, D), row2(final_norm), tm_proj).reshape(Bn, S, D)
```

```python
import functools
import math

import jax
import jax.numpy as jnp
from jax import lax
from jax.experimental import pallas as pl
from jax.experimental.pallas import tpu as pltpu

F32 = jnp.float32
BF16 = jnp.bfloat16
EPS = 1e-6

LANES = 128
BF16_ROWS = 16
VMEM_LIMIT = 56 * 1024 * 1024

GDN_HEADS = 4
GDN_CHUNK = 64
GDN_CONV = 4
GDN_INV_PASSES = 1
LRU_CONV = 4
S5_GROUP = 16
S5_STATE = 64
LRU_C = 8.0
XA_HEADS = 4

HI = lax.Precision.HIGHEST


def _cparams(sem):
    return pltpu.CompilerParams(dimension_semantics=sem, vmem_limit_bytes=VMEM_LIMIT)


def _rms(x, g):
    return x * lax.rsqrt(jnp.mean(x * x, axis=-1, keepdims=True) + EPS) * g


def _dot(a, b):
    return jnp.dot(a, b, preferred_element_type=F32)


def _dot_nt(a, b):
    return lax.dot_general(a, b, (((1,), (1,)), ((), ())), preferred_element_type=F32)


def _sigmoid(x):
    return 0.5 * jnp.tanh(0.5 * x) + 0.5


def _softplus(x):
    return jnp.maximum(x, 0.0) + jnp.log(1.0 + jnp.exp(-jnp.abs(x)))


def _const_spec(shape):
    nd = len(shape)
    return pl.BlockSpec(shape, lambda *_: (0,) * nd)


def _proj_kernel(x_ref, xp_ref, xn_ref, g_ref, w_ref, wq_ref, wab_ref, cw_ref, p_ref, ab_ref, h_ref, hh_ref,
                 q_ref, *, tiles_per_seq, norm_chunks):
    i = pl.program_id(0)
    j = pl.program_id(1)
    last = pl.num_programs(1) - 1
    tm = x_ref.shape[0]
    hal = xp_ref.shape[0]
    ext = tm + 2 * hal
    n_chunk = wq_ref.shape[0]
    per_step = 2
    busy_steps = n_chunk // per_step

    @pl.when(j == 0)
    def _():
        g = g_ref[...]
        hb = _rms(x_ref[...], g).astype(BF16)
        h_ref[...] = hb
        ab_ref[...] = _dot(hb, wab_ref[...])
        si = i % tiles_per_seq
        hp = _rms(xp_ref[...], g) * (si > 0).astype(F32)
        hn = _rms(xn_ref[...], g) * (si < tiles_per_seq - 1).astype(F32)
        hh_ref[...] = jnp.concatenate([hp, hn], axis=0).astype(BF16)

    def plain(n):
        cols = slice(n * 2 * LANES, (n + 1) * 2 * LANES)
        p_ref[:, cols] = _dot(h_ref[...], w_ref[:, cols]).astype(BF16)

    def chunk_matmul(c):
        um = _dot(h_ref[...], wq_ref[c])
        uh = _dot(hh_ref[...], wq_ref[c])
        return jnp.concatenate([uh[:hal], um, uh[hal:]], axis=0)

    def chunk_finish(c, ue, k):
        ls = slice(k * LANES, (k + 1) * LANES)
        u = ue[:, ls]
        w = cw_ref[c][:, ls]
        y = (w[0:1] * pltpu.roll(u, 2, 0)[hal:hal + tm] + w[1:2] * pltpu.roll(u, 1, 0)[hal:hal + tm]
             + w[2:3] * u[hal:hal + tm] + w[3:4] * pltpu.roll(u, ext - 1, 0)[hal:hal + tm])
        y = y * _sigmoid(y)
        yn = y * lax.rsqrt(jnp.sum(y * y, axis=-1, keepdims=True) + EPS)
        q_ref[c, :, ls] = jnp.where(c < norm_chunks, yn, y).astype(BF16)

    @pl.when(j < busy_steps)
    def _():
        n_plain = p_ref.shape[1] // (2 * LANES)
        todo = list(range(n_plain))
        for k in range(per_step):
            c = j * per_step + k
            ue = chunk_matmul(c)
            for half in range(2):
                if todo:
                    plain(todo.pop(0))
                chunk_finish(c, ue, half)
        for n in todo:
            plain(n)

    @pl.when((j >= busy_steps) & (j < last))
    def _():
        p_ref[...] = _dot(h_ref[...], w_ref[...]).astype(BF16)

    @pl.when(j == last)
    def _():
        for c in range(n_chunk):
            p_ref[:, c * 2 * LANES:(c + 1) * 2 * LANES] = q_ref[c]


def _proj(x2, gain, w_main, w_ab, conv_w, seq, tm):
    T, D = x2.shape
    N = w_main.shape[1]
    tn = conv_w.shape[1]
    nj = N // tn
    chunk = 2 * LANES
    n_chunk = tn // chunk
    hal = 8
    nb = tm // hal
    assert N % tn == 0 and seq % tm == 0 and tn % chunk == 0 and n_chunk % 2 == 0 and n_chunk // 2 < nj
    assert (2 * GDN_HEADS * LANES) % chunk == 0
    wq = w_main[:, :tn].reshape(D, n_chunk, chunk).transpose(1, 0, 2)
    cw = conv_w.reshape(conv_w.shape[0], n_chunk, chunk).transpose(1, 0, 2)
    col = lambda j: (j + 1) % nj
    return pl.pallas_call(
        functools.partial(_proj_kernel, tiles_per_seq=seq // tm, norm_chunks=2 * GDN_HEADS * LANES // chunk),
        grid=(T // tm, nj),
        in_specs=[
            pl.BlockSpec((tm, D), lambda i, j: (i, 0)),
            pl.BlockSpec((hal, D), lambda i, j: (jnp.maximum(i * nb - 1, 0), 0)),
            pl.BlockSpec((hal, D), lambda i, j: (jnp.minimum((i + 1) * nb, T // hal - 1), 0)),
            pl.BlockSpec((1, D), lambda i, j: (0, 0)),
            pl.BlockSpec((D, tn), lambda i, j: (0, jnp.minimum(j + 1, nj - 1))),
            _const_spec(wq.shape),
            pl.BlockSpec((D, LANES), lambda i, j: (0, 0)),
            _const_spec(cw.shape),
        ],
        out_specs=[
            pl.BlockSpec((tm, tn), lambda i, j: (i, col(j))),
            pl.BlockSpec((tm, LANES), lambda i, j: (i, 0)),
        ],
        out_shape=[jax.ShapeDtypeStruct((T, N), BF16), jax.ShapeDtypeStruct((T, LANES), F32)],
        scratch_shapes=[pltpu.VMEM((tm, D), BF16), pltpu.VMEM((2 * hal, D), BF16),
                        pltpu.VMEM((n_chunk, tm, chunk), BF16)],
        compiler_params=_cparams(("parallel", "arbitrary")),
        name="proj",
    )(x2, x2, x2, gain, w_main, wq, w_ab, cw)


def _mm_split(lhs, w, n_pass):
    lh, wh = lhs.astype(BF16), w.astype(BF16)
    if n_pass == 1:
        return _dot(lh, wh)
    ll = (lhs - lh.astype(F32)).astype(BF16)
    wl = (w - wh.astype(F32)).astype(BF16)
    return _dot(lh, wh) + (_dot(lh, wl) + _dot(ll, wh))


def _gdn_prepare(d, qc, gam, beta, grow, msk):
    C = GDN_CHUNK
    H = GDN_HEADS
    Dh = LANES
    blk, incl, strict, eye = msk
    stack = lambda off: jnp.concatenate([qc[:, off + h * Dh: off + (h + 1) * Dh] for h in range(H)], axis=0)
    Qs = stack(0).astype(F32) * (Dh ** -0.5)
    Kb16 = stack(H * Dh)
    Ks = Kb16.astype(F32)
    Vs = stack(2 * H * Dh).astype(F32)
    gcol = jnp.concatenate(gam, axis=0)
    bcol = jnp.concatenate(beta, axis=0)
    edge = C - 1 if d == 0 else 0
    glast = [g[edge:edge + 1] for g in gam]
    glast_col = jnp.concatenate([jnp.broadcast_to(g, (C, 1)) for g in glast], axis=0)

    diff = _side_by_side(gam, blk) - grow
    decay = jnp.where(incl, jnp.exp(jnp.where(incl, diff, 0.0)), 0.0)
    kb = Ks * bcol
    g2 = _dot_nt(jnp.concatenate([kb, Qs], axis=0).astype(BF16), Kb16)
    kk = _side_by_side([g2[h * C:(h + 1) * C] for h in range(H)], blk)
    qk = _side_by_side([g2[(H + h) * C:(H + h + 1) * C] for h in range(H)], blk) * decay
    X = jnp.where(strict, -(kk * decay), 0.0)
    egam = jnp.exp(gcol)
    return dict(X=X, P=eye + X, qk=qk, rhs=jnp.concatenate([Vs * bcol, kb * egam], axis=1).astype(BF16),
                qdec=Qs * egam, kdec=Ks * jnp.exp(glast_col - gcol), eg=[jnp.exp(g) for g in glast])


def _side_by_side(cols, blk):
    out = jnp.where(blk[0], cols[0], 0.0)
    for h in range(1, len(blk)):
        out = jnp.where(blk[h], cols[h], out)
    return out


def _bd(m, blk):
    return jnp.concatenate([jnp.where(b, m, 0.0) for b in blk], axis=0)


def _gdn_inverse_step(c, k, blk):
    C = GDN_CHUNK
    w = _bd(c["X"], blk)
    if k == 0:
        c["X"] = _mm_split(c["X"], w, GDN_INV_PASSES)
    elif k < 5:
        r = _mm_split(jnp.concatenate([c["P"], c["X"]], axis=0), w, GDN_INV_PASSES)
        c["P"] = c["P"] + r[:C]
        c["X"] = r[C:]
    else:
        c["P"] = c["P"] + _mm_split(c["P"], w, GDN_INV_PASSES)


def _gdn_wy(c, blk, kblk):
    C = GDN_CHUNK
    H = GDN_HEADS
    wy = _dot(_bd(c["P"], blk).astype(BF16), c["rhs"])
    c["U"], Wm = wy[:, :LANES], wy[:, LANES:]
    qdec = c["qdec"]
    c["lhs"] = [jnp.concatenate([Wm[h * C:(h + 1) * C], qdec[h * C:(h + 1) * C]], axis=0).astype(BF16)
                for h in range(H)]
    kdecT = c["kdec"].T
    c["kT"] = [jnp.where(kblk == h, kdecT, 0.0).astype(BF16) for h in range(H)]
    c["qkbd"] = _bd(c["qk"], blk).astype(BF16)


def _gdn_kernel(qf_ref, af_ref, qb_ref, abk_ref, prm_ref, of_ref, ob_ref, sf_ref, sb_ref):
    C = GDN_CHUNK
    H = GDN_HEADS
    CT = qf_ref.shape[1]
    nch = CT // C

    @pl.when(pl.program_id(1) == 0)
    def _():
        sf_ref[...] = jnp.zeros_like(sf_ref)
        sb_ref[...] = jnp.zeros_like(sb_ref)

    ri = lax.broadcasted_iota(jnp.int32, (C, H * C), 0)
    lane = lax.broadcasted_iota(jnp.int32, (C, H * C), 1)
    cj = lane % C
    blk = [(lane // C) == h for h in range(H)]
    eye = (cj == ri).astype(F32)
    kblk = lax.broadcasted_iota(jnp.int32, (LANES, H * C), 1) // C
    rt = lax.broadcasted_iota(jnp.int32, (CT, CT), 0)
    ct = lax.broadcasted_iota(jnp.int32, (CT, CT), 1)
    same_t = (rt // C) == (ct // C)
    neg_a = -jnp.exp(prm_ref[0:1, :])
    dt_b = prm_ref[1:2, :]

    dirs = ((0, qf_ref, af_ref, of_ref, sf_ref), (1, qb_ref, abk_ref, ob_ref, sb_ref))

    chunks = {}
    for d, q_ref, a_ref, o_ref, s_ref in dirs:
        ab = a_ref[0]
        bt = _sigmoid(ab)
        la = neg_a * _softplus(ab + dt_b)
        cum = (same_t & ((ct <= rt) if d == 0 else (ct >= rt))).astype(F32)
        gam = jnp.dot(cum, la, precision=HI, preferred_element_type=F32)
        slab = gam.T[2 * H:4 * H, :]
        msk = (blk, (cj <= ri) if d == 0 else (cj >= ri), (cj < ri) if d == 0 else (cj > ri), eye)
        for ci in range(nch):
            rows = slice(ci * C, (ci + 1) * C)
            gl = [gam[rows, 2 * H + d * H + h: 2 * H + d * H + h + 1] for h in range(H)]
            bl = [bt[rows, d * H + h: d * H + h + 1] for h in range(H)]
            pieces = []
            for h in range(H):
                shift = ((h - ci) * C) % CT
                moved = slab if shift == 0 else pltpu.roll(slab, shift, 1)
                pieces.append(moved[d * H + h:d * H + h + 1, :])
            grow = _side_by_side(pieces, [b[0:1, :] for b in blk])
            chunks[d, ci] = _gdn_prepare(d, q_ref[0, rows, :], gl, bl, grow, msk)

    order = [(d, step if d == 0 else nch - 1 - step) for step in range(nch) for d in (0, 1)]
    for k in range(6):
        for key in order:
            _gdn_inverse_step(chunks[key], k, blk)
    for key in order:
        _gdn_wy(chunks[key], blk, kblk)

    state = {d: [s_ref[h] for h in range(H)] for d, _, _, _, s_ref in dirs}
    for step in range(nch):
        cur = {d: chunks[d, step if d == 0 else nch - 1 - step] for d in (0, 1)}
        r = {d: [_dot(cur[d]["lhs"][h], state[d][h].astype(BF16)) for h in range(H)] for d in (0, 1)}
        vb = {}
        for d, _, _, o_ref, _ in dirs:
            c = cur[d]
            vnew = jnp.concatenate([c["U"][h * C:(h + 1) * C] - r[d][h][:C] for h in range(H)], axis=0)
            vb[d] = vnew.astype(BF16)
            out = jnp.concatenate([r[d][h][C:] for h in range(H)], axis=0) + _dot(c["qkbd"], vb[d])
            ci = step if d == 0 else nch - 1 - step
            for h in range(H):
                o_ref[0, ci * C:(ci + 1) * C, h * LANES:(h + 1) * LANES] = out[h * C:(h + 1) * C].astype(o_ref.dtype)
        for d in (0, 1):
            for h in range(H):
                state[d][h] = state[d][h] * cur[d]["eg"][h] + _dot(cur[d]["kT"][h], vb[d])
    for d, _, _, _, s_ref in dirs:
        for h in range(H):
            s_ref[h] = state[d][h]


def _gdn(qkv, ab3, prm, ct, C3):
    Bn, S, _ = qkv.shape
    W = C3 // 3
    nt = S // ct
    dh = W // GDN_HEADS
    assert ct == GDN_HEADS * GDN_CHUNK and dh == LANES
    fwd = lambda b, n: (b, n, 0)
    bwd = lambda b, n: (b, nt - 1 - n, 0)
    return pl.pallas_call(
        _gdn_kernel,
        grid=(Bn, nt),
        in_specs=[
            pl.BlockSpec((1, ct, C3), fwd),
            pl.BlockSpec((1, ct, LANES), fwd),
            pl.BlockSpec((1, ct, C3), bwd),
            pl.BlockSpec((1, ct, LANES), bwd),
            _const_spec(prm.shape),
        ],
        out_specs=[pl.BlockSpec((1, ct, W), fwd), pl.BlockSpec((1, ct, W), bwd)],
        out_shape=[jax.ShapeDtypeStruct((Bn, S, W), BF16)] * 2,
        scratch_shapes=[pltpu.VMEM((GDN_HEADS, dh, dh), F32)] * 2,
        compiler_params=_cparams(("parallel", "arbitrary")),
        name="gdn",
    )(qkv, ab3, qkv, ab3, prm)


def _s5_kernel(uf_ref, ub_ref, bw_ref, cw_ref, a_ref, yf_ref, yb_ref, bu_ref, hb_ref, carry_ref, *, nb):
    R = uf_ref.shape[0]
    tt = R // nb
    nq = bw_ref.shape[1]
    ns = bw_ref.shape[3]
    half = ns // 2
    n_tile = 2 * LANES
    u_refs, y_refs = (uf_ref, ub_ref), (yf_ref, yb_ref)
    units = [(d, q) for d in (0, 1) for q in range(nq)]

    @pl.when(pl.program_id(0) == 0)
    def _():
        carry_ref[...] = jnp.zeros_like(carry_ref)

    def expand(k, n):
        d, q = units[k]
        cols = slice(n * n_tile, (n + 1) * n_tile)
        bu_ref[k % 2, :, cols] = _dot(u_refs[d][:, q * LANES:(q + 1) * LANES], bw_ref[d, q, :, cols])

    def contract(k, part, parts=2):
        d, q = units[k]
        rows = slice(part * R // parts, (part + 1) * R // parts)
        y_refs[d][rows, q * LANES:(q + 1) * LANES] = _dot(hb_ref[k % 2, rows, :], cw_ref[d, q]).astype(y_refs[d].dtype)

    for n in range(ns // n_tile):
        expand(0, n)
    for k, (d, q) in enumerate(units):
        mxu = []
        if k + 1 < len(units):
            mxu += [functools.partial(expand, k + 1, n) for n in range(ns // n_tile)]
        if k > 0:
            mxu += [functools.partial(contract, k - 1, p) for p in range(2)]
        every = -(-tt // len(mxu))
        ar = a_ref[d, 0, :, q * half:(q + 1) * half]
        ai = a_ref[d, 1, :, q * half:(q + 1) * half]
        cre = slice(q * ns, q * ns + half)
        cim = slice(q * ns + half, (q + 1) * ns)
        hr, hi = carry_ref[d, :, cre], carry_ref[d, :, cim]
        for s in range(tt):
            if mxu and s % every == 0:
                mxu.pop(0)()
            t = s if d == 0 else tt - 1 - s
            rows = slice(t * nb, (t + 1) * nb)
            hr, hi = (ar * hr - ai * hi + bu_ref[k % 2, rows, :half],
                      ar * hi + ai * hr + bu_ref[k % 2, rows, half:])
            hb_ref[k % 2, rows, :half] = hr.astype(BF16)
            hb_ref[k % 2, rows, half:] = hi.astype(BF16)
        for piece in mxu:
            piece()
        carry_ref[d, :, cre] = hr
        carry_ref[d, :, cim] = hi
    for p in range(2):
        contract(len(units) - 1, p)


def _s5(utm, bw, cw, a_b, nb, rows, col_blk):
    TB = utm.shape[0]
    nq = bw.shape[1]
    W = nq * LANES
    nstate = bw.shape[3] * nq
    nt = TB // rows
    fwd = lambda n: (n, col_blk)
    bwd = lambda n: (nt - 1 - n, col_blk)
    return pl.pallas_call(
        functools.partial(_s5_kernel, nb=nb),
        grid=(nt,),
        in_specs=[
            pl.BlockSpec((rows, W), fwd),
            pl.BlockSpec((rows, W), bwd),
            _const_spec(bw.shape),
            _const_spec(cw.shape),
            _const_spec(a_b.shape),
        ],
        out_specs=[pl.BlockSpec((rows, W), lambda n: (n, 0)), pl.BlockSpec((rows, W), lambda n: (nt - 1 - n, 0))],
        out_shape=[jax.ShapeDtypeStruct((TB, W), BF16)] * 2,
        scratch_shapes=[pltpu.VMEM((2, rows, bw.shape[3]), F32), pltpu.VMEM((2, rows, bw.shape[3]), BF16),
                        pltpu.VMEM((2, nb, nstate), F32)],
        compiler_params=_cparams(("arbitrary",)),
        name="s5_scan",
    )(utm, utm, bw, cw, a_b)


def _lru_kernel(xf_ref, xfp_ref, xfn_ref, xb_ref, xbp_ref, xbn_ref, cw_ref, cb_ref, wg_ref, bg_ref, lam_ref,
                of_ref, ob_ref, a_s, b_s, carry_ref, *, nb):
    n = pl.program_id(0)
    nt = pl.num_programs(0)
    R = xf_ref.shape[0]
    W = xf_ref.shape[1]
    tt = R // nb

    @pl.when(n == 0)
    def _():
        carry_ref[...] = jnp.zeros_like(carry_ref)

    streams = ((0, n, xf_ref, xfp_ref, xfn_ref, of_ref), (1, nt - 1 - n, xb_ref, xbp_ref, xbn_ref, ob_ref))
    for d, pos, xm_ref, xp_ref, xn_ref, o_ref in streams:
        xp = xp_ref[...].astype(F32) * (pos > 0).astype(F32)
        xn = xn_ref[...].astype(F32) * (pos < nt - 1).astype(F32)
        xe = jnp.concatenate([xp, xm_ref[...].astype(F32), xn], axis=0)
        w = cw_ref[...]
        xc = cb_ref[...] + sum(w[k:k + 1] * xe[k * nb:k * nb + R] for k in range(LRU_CONV))
        gates = _dot(xc.astype(BF16), wg_ref[d]) + bg_ref[d]
        rg = _sigmoid(gates[:, :W])
        ig = _sigmoid(gates[:, W:])
        a = jnp.exp(-LRU_C * rg * _softplus(-lam_ref[d]))
        a_s[d] = a
        b_s[d] = jnp.sqrt(1.0 - a * a) * (ig * xc)

        def body(s, h, d=d, o_ref=o_ref):
            t = s if d == 0 else tt - 1 - s
            rows = pl.ds(pl.multiple_of(t * nb, nb), nb)
            h = a_s[d, rows, :] * h + b_s[d, rows, :]
            o_ref[rows, :] = h.astype(o_ref.dtype)
            return h

        carry_ref[d] = lax.fori_loop(0, tt, body, carry_ref[d], unroll=4)


def _lru(utm, conv_w, conv_b, wg, bg, lam, nb, rows, col_blk):
    TB = utm.shape[0]
    W = conv_w.shape[1]
    nt = TB // rows
    hp = 2 * nb
    hn = nb
    nbt = TB // nb

    def specs(pos):
        return [
            pl.BlockSpec((rows, W), lambda n: (pos(n), col_blk)),
            pl.BlockSpec((hp, W), lambda n: (jnp.maximum(pos(n) * (rows // hp) - 1, 0), col_blk)),
            pl.BlockSpec((hn, W), lambda n: (jnp.minimum((pos(n) + 1) * (rows // hn), nbt - 1), col_blk)),
        ]

    fwd = lambda n: n
    bwd = lambda n: nt - 1 - n
    return pl.pallas_call(
        functools.partial(_lru_kernel, nb=nb),
        grid=(nt,),
        in_specs=specs(fwd) + specs(bwd) + [_const_spec(conv_w.shape), _const_spec(conv_b.shape),
                                            _const_spec(wg.shape), _const_spec(bg.shape), _const_spec(lam.shape)],
        out_specs=[pl.BlockSpec((rows, W), lambda n: (n, 0)), pl.BlockSpec((rows, W), lambda n: (nt - 1 - n, 0))],
        out_shape=[jax.ShapeDtypeStruct((TB, W), BF16)] * 2,
        scratch_shapes=[pltpu.VMEM((2, rows, W), F32), pltpu.VMEM((2, rows, W), F32), pltpu.VMEM((2, nb, W), F32)],
        compiler_params=_cparams(("arbitrary",)),
        name="lru_scan",
    )(utm, utm, utm, utm, utm, utm, conv_w, conv_b, wg, bg, lam)


def _post_kernel(u_ref, g_ref, yf_ref, yb_ref, hf_ref, hb_ref, d_ref, gw_ref, gb_ref, o_ref):
    W = u_ref.shape[1]
    y = u_ref[...].astype(F32) * d_ref[...] + yf_ref[...].astype(F32) + yb_ref[...].astype(F32)
    zg = jax.nn.gelu(y)
    o_ref[:, :W] = (zg * _sigmoid(_dot(zg.astype(BF16), gw_ref[...]) + gb_ref[...])).astype(o_ref.dtype)
    h = hf_ref[...].astype(F32) + hb_ref[...].astype(F32)
    o_ref[:, W:] = (h * jax.nn.gelu(g_ref[...].astype(F32))).astype(o_ref.dtype)


def _post(utm, yf, yb, hf, hb, s5_d, glu_w, glu_b, rows, u_blk, g_blk):
    TB, W = yf.shape
    row = lambda n: (n, 0)
    return pl.pallas_call(
        _post_kernel,
        grid=(TB // rows,),
        in_specs=[
            pl.BlockSpec((rows, W), lambda n: (n, u_blk)),
            pl.BlockSpec((rows, W), lambda n: (n, g_blk)),
            pl.BlockSpec((rows, W), row), pl.BlockSpec((rows, W), row),
            pl.BlockSpec((rows, W), row), pl.BlockSpec((rows, W), row),
            _const_spec(s5_d.shape), _const_spec(glu_w.shape), _const_spec(glu_b.shape),
        ],
        out_specs=pl.BlockSpec((rows, 2 * W), row),
        out_shape=jax.ShapeDtypeStruct((TB, 2 * W), BF16),
        compiler_params=_cparams(("parallel",)),
        name="s5_lru_post",
    )(utm, utm, yf, yb, hf, hb, s5_d, glu_w, glu_b)


def _merge_kernel(x_ref, of_ref, ob_ref, z_ref, sl_ref, scb_ref, scc_ref, scx_ref, ccp_ref, cxp_ref, ccn_ref,
                  cxn_ref, g0_ref, g1_ref, g2_ref, g3_ref, gn_ref, scw_ref, wb_ref, wo_ref, o_ref):
    i = pl.program_id(1)
    last = pl.num_programs(1) - 1
    tm = x_ref.shape[1]
    W = z_ref.shape[2]
    hal = ccp_ref.shape[1]

    o = of_ref[0].astype(F32) + ob_ref[0].astype(F32)
    z = z_ref[0].astype(F32)
    ys = []
    for h in range(GDN_HEADS):
        oh = o[:, h * LANES:(h + 1) * LANES]
        ys.append(oh * lax.rsqrt(jnp.mean(oh * oh, axis=-1, keepdims=True) + EPS) * gn_ref[...])
    y_gdn = jnp.concatenate(ys, axis=1) * (z * _sigmoid(z))

    cx = scc_ref[0].astype(F32) * scx_ref[0].astype(F32)
    cx_p = (ccp_ref[0, hal - 1:hal].astype(F32) * cxp_ref[0, hal - 1:hal].astype(F32)) * (i > 0).astype(F32)
    cx_n = (ccn_ref[0, 0:1].astype(F32) * cxn_ref[0, 0:1].astype(F32)) * (i < last).astype(F32)
    row = lax.broadcasted_iota(jnp.int32, (tm, 1), 0)
    m1 = jnp.where(row == 0, cx_p, pltpu.roll(cx, 1, 0))
    p1 = jnp.where(row == tm - 1, cx_n, pltpu.roll(cx, tm - 1, 0))
    w = scw_ref[...]
    y_sc = scb_ref[0].astype(F32) * (w[0:1] * m1 + w[1:2] * cx + w[2:3] * p1)

    branches = (y_gdn.astype(BF16), sl_ref[0, :, :W], sl_ref[0, :, W:], y_sc.astype(BF16))
    merged = None
    for m, (y, g_ref) in enumerate(zip(branches, (g0_ref, g1_ref, g2_ref, g3_ref))):
        t = _sigmoid(g_ref[0].astype(F32)) * _dot(y, wb_ref[m])
        merged = t if merged is None else merged + t
    o_ref[0] = x_ref[0] + _dot(merged.astype(BF16), wo_ref[...])


def _merge(x3, o_f, o_b, P3, sl3, gdn_gain, sc_w, w_branch, w_out, tm, blk):
    Bn, S, D = x3.shape
    W = o_f.shape[2]
    hal = BF16_ROWS
    nb = tm // hal
    tok = lambda b, i: (b, i, 0)
    col = lambda k: (lambda b, i: (b, i, k))
    prev = lambda k: (lambda b, i: (b, jnp.maximum(i * nb - 1, 0), k))
    nxt = lambda k: (lambda b, i: (b, jnp.minimum((i + 1) * nb, S // hal - 1), k))
    gate0 = blk["gates"] * W // D
    return pl.pallas_call(
        _merge_kernel,
        grid=(Bn, S // tm),
        in_specs=[
            pl.BlockSpec((1, tm, D), tok),
            pl.BlockSpec((1, tm, W), tok), pl.BlockSpec((1, tm, W), tok),
            pl.BlockSpec((1, tm, W), col(blk["z"])),
            pl.BlockSpec((1, tm, 2 * W), tok),
            pl.BlockSpec((1, tm, W), col(blk["sc_b"])),
            pl.BlockSpec((1, tm, W), col(blk["sc_c"])),
            pl.BlockSpec((1, tm, W), col(blk["sc_x"])),
            pl.BlockSpec((1, hal, W), prev(blk["sc_c"])), pl.BlockSpec((1, hal, W), prev(blk["sc_x"])),
            pl.BlockSpec((1, hal, W), nxt(blk["sc_c"])), pl.BlockSpec((1, hal, W), nxt(blk["sc_x"])),
            pl.BlockSpec((1, tm, D), col(gate0)), pl.BlockSpec((1, tm, D), col(gate0 + 1)),
            pl.BlockSpec((1, tm, D), col(gate0 + 2)), pl.BlockSpec((1, tm, D), col(gate0 + 3)),
            _const_spec(gdn_gain.shape), _const_spec(sc_w.shape), _const_spec(w_branch.shape),
            _const_spec(w_out.shape),
        ],
        out_specs=pl.BlockSpec((1, tm, D), tok),
        out_shape=jax.ShapeDtypeStruct((Bn, S, D), F32),
        compiler_params=_cparams(("parallel", "parallel")),
        name="merge",
    )(x3, o_f, o_b, P3, sl3, P3, P3, P3, P3, P3, P3, P3, P3, P3, P3, P3, gdn_gain, sc_w, w_branch, w_out)


def _kv_kernel(m_ref, g_ref, w_ref, o_ref):
    o_ref[0] = _dot(_rms(m_ref[0], g_ref[...]).astype(BF16), w_ref[...]).astype(o_ref.dtype)


def _kv(mem, gain, w_kv):
    Bn, M, D = mem.shape
    return pl.pallas_call(
        _kv_kernel,
        grid=(Bn,),
        in_specs=[pl.BlockSpec((1, M, D), lambda b: (b, 0, 0)), _const_spec(gain.shape), _const_spec(w_kv.shape)],
        out_specs=pl.BlockSpec((1, M, 2 * D), lambda b: (b, 0, 0)),
        out_shape=jax.ShapeDtypeStruct((Bn, M, 2 * D), BF16),
        compiler_params=_cparams(("parallel",)),
        name="xa_kv",
    )(mem, gain, w_kv)


def _xa_kernel(x_ref, kv_ref, g_ref, wq_ref, wo_ref, o_ref):
    D = x_ref.shape[2]
    dh = D // XA_HEADS
    x = x_ref[0]
    q = _dot(_rms(x, g_ref[...]).astype(BF16), wq_ref[...]).astype(BF16)
    outs = []
    for h in range(XA_HEADS):
        k = kv_ref[0, :, h * dh:(h + 1) * dh]
        v = kv_ref[0, :, D + h * dh:D + (h + 1) * dh]
        s = _dot_nt(q[:, h * dh:(h + 1) * dh], k) * (dh ** -0.5)
        e = jnp.exp(s - jnp.max(s, axis=-1, keepdims=True))
        p = e / jnp.sum(e, axis=-1, keepdims=True)
        outs.append(_dot(p.astype(BF16), v))
    o = jnp.concatenate(outs, axis=1).astype(BF16)
    o_ref[0] = x + _dot(o, wo_ref[...])


def _xa(x3, kv, gain, wq, wo, tm):
    Bn, S, D = x3.shape
    M = kv.shape[1]
    tok = lambda b, i: (b, i, 0)
    return pl.pallas_call(
        _xa_kernel,
        grid=(Bn, S // tm),
        in_specs=[pl.BlockSpec((1, tm, D), tok), pl.BlockSpec((1, M, 2 * D), lambda b, i: (b, 0, 0)),
                  _const_spec(gain.shape), _const_spec(wq.shape), _const_spec(wo.shape)],
        out_specs=pl.BlockSpec((1, tm, D), tok),
        out_shape=jax.ShapeDtypeStruct((Bn, S, D), F32),
        compiler_params=_cparams(("parallel", "parallel")),
        name="xattn",
    )(x3, kv, gain, wq, wo)


def _ffn_kernel(x_ref, xp_ref, xn_ref, g_ref, wu_ref, cw_ref, cb_ref, wd_ref, o_ref, act_ref, *, tf):
    i = pl.program_id(1)
    last = pl.num_programs(1) - 1
    tm = x_ref.shape[1]
    hal = xp_ref.shape[1]
    F = wd_ref.shape[0]
    x = x_ref[0]
    g = g_ref[...]
    hp = _rms(xp_ref[0], g) * (i > 0).astype(F32)
    hn = _rms(xn_ref[0], g) * (i < last).astype(F32)
    he = jnp.concatenate([hp, _rms(x, g), hn], axis=0).astype(BF16)
    ext = tm + 2 * hal

    def conv(u, cols):
        w = cw_ref[:, cols]
        m1 = pltpu.roll(u, 1, 0)[hal:hal + tm]
        p1 = pltpu.roll(u, ext - 1, 0)[hal:hal + tm]
        return w[0:1] * m1 + w[1:2] * u[hal:hal + tm] + w[2:3] * p1 + cb_ref[:, cols]

    for f in range(F // tf):
        gc = slice(f * tf, (f + 1) * tf)
        uc = slice(F + f * tf, F + (f + 1) * tf)
        gate = conv(_dot(he, wu_ref[:, gc]), gc)
        up = conv(_dot(he, wu_ref[:, uc]), uc)
        act_ref[:, gc] = (gate * _sigmoid(gate) * up).astype(BF16)
    o_ref[0] = x + _dot(act_ref[...], wd_ref[...])


def _ffn(x3, gain, w_up, conv_w, conv_b, w_down, tm, tf):
    Bn, S, D = x3.shape
    hal = 8
    nb = tm // hal
    tok = lambda b, i: (b, i, 0)
    resident = lambda shape: pl.BlockSpec(shape, lambda b, i: (0, 0), pipeline_mode=pl.Buffered(1))
    return pl.pallas_call(
        functools.partial(_ffn_kernel, tf=tf),
        grid=(Bn, S // tm),
        in_specs=[
            pl.BlockSpec((1, tm, D), tok),
            pl.BlockSpec((1, hal, D), lambda b, i: (b, jnp.maximum(i * nb - 1, 0), 0)),
            pl.BlockSpec((1, hal, D), lambda b, i: (b, jnp.minimum((i + 1) * nb, S // hal - 1), 0)),
            _const_spec(gain.shape), resident(w_up.shape), _const_spec(conv_w.shape), _const_spec(conv_b.shape),
            resident(w_down.shape),
        ],
        out_specs=pl.BlockSpec((1, tm, D), tok),
        out_shape=jax.ShapeDtypeStruct((Bn, S, D), F32),
        scratch_shapes=[pltpu.VMEM((tm, w_down.shape[0]), BF16)],
        compiler_params=_cparams(("parallel", "parallel")),
        name="ffn",
    )(x3, x3, x3, gain, w_up, conv_w, conv_b, w_down)


def _norm_kernel(x_ref, g_ref, o_ref):
    o_ref[...] = _rms(x_ref[...], g_ref[...])


def _final_norm(x2, gain, tm):
    T, D = x2.shape
    return pl.pallas_call(
        _norm_kernel,
        grid=(T // tm,),
        in_specs=[pl.BlockSpec((tm, D), lambda i: (i, 0)), _const_spec(gain.shape)],
        out_specs=pl.BlockSpec((tm, D), lambda i: (i, 0)),
        out_shape=jax.ShapeDtypeStruct((T, D), F32),
        compiler_params=_cparams(("parallel",)),
        name="final_norm",
    )(x2, gain)


def _s5_weights(lam_re, lam_im, log_step, b_re, b_im, c_re, c_im, nb):
    G, P, J = b_re.shape[1:]
    gq = LANES // J
    nq = G // gq
    lr = jnp.minimum(lam_re, -1e-4)
    li = lam_im
    step = jnp.exp(log_step)[..., None]
    mag = jnp.exp(lr * step)
    a_re, a_im = mag * jnp.cos(li * step), mag * jnp.sin(li * step)
    den = lr * lr + li * li
    f_re = ((a_re - 1.0) * lr + a_im * li) / den
    f_im = (a_im * lr - (a_re - 1.0) * li) / den
    bb_re = f_re[..., None] * b_re - f_im[..., None] * b_im
    bb_im = f_re[..., None] * b_im + f_im[..., None] * b_re
    eye = jnp.eye(gq, dtype=F32)

    def expand(bb):
        t = bb.reshape(2, nq, gq, P, J)
        return jnp.einsum("dqgpj,gh->dqgjhp", t, eye).reshape(2, nq, gq * J, gq * P)

    def contract(cc):
        t = cc.reshape(2, nq, gq, J, P)
        return jnp.einsum("dqgjp,gh->dqgphj", t, eye).reshape(2, nq, gq * P, gq * J)

    bw = jnp.concatenate([expand(bb_re), expand(bb_im)], axis=3).astype(BF16)
    cw = jnp.concatenate([contract(c_re), -contract(c_im)], axis=2).astype(BF16)
    a_b = jnp.stack([a_re.reshape(2, G * P), a_im.reshape(2, G * P)], axis=1)
    a_b = jnp.broadcast_to(a_b[:, :, None, :], (2, 2, nb, G * P))
    return bw, cw, a_b


def _lru_gate_weights(wa, wx, ba, bx):
    nblk, bs = wa.shape[1], wa.shape[2]
    eye = jnp.eye(nblk, dtype=F32)
    dense = lambda w: jnp.einsum("dnkm,nl->dnklm", w, eye).reshape(2, nblk * bs, nblk * bs)
    wg = jnp.concatenate([dense(wa), dense(wx)], axis=2).astype(BF16)
    bg = jnp.concatenate([ba, bx], axis=1)[:, None, :]
    return wg, bg


def _pick(n, pref):
    return pref if n % pref == 0 else n


def kernel(x, mem, mix_norm, w_in, gdn_conv, gdn_a_log, gdn_dt_bias, gdn_out_norm, s5_lambda_re, s5_lambda_im, s5_log_step, s5_b_re, s5_b_im, s5_c_re, s5_c_im, s5_d, s5_glu_w, s5_glu_b, lru_conv_w, lru_conv_b, lru_gate_a_w, lru_gate_a_b, lru_gate_x_w, lru_gate_x_b, lru_lambda, sc_conv, w_branch, w_mix_out, xa_norm, xa_mem_norm, xa_w_q, xa_w_kv, xa_w_o, ffn_norm, ffn_w_up, ffn_conv_w, ffn_conv_b, ffn_w_down, final_norm):
    Bn, S, D = x.shape
    depth = w_in.shape[0]
    W = D // 2
    H = GDN_HEADS
    T = Bn * S
    F = ffn_w_down.shape[1]
    assert Bn % BF16_ROWS == 0 and S % 256 == 0 and W == H * LANES

    blk = {"z": 3, "s5_u": 4, "lru_x": 5, "lru_g": 6, "sc_b": 7, "sc_c": 8, "sc_x": 9, "gates": 10}
    n_ab = 4 * H
    c_ab = 4 * W
    tm_proj = _pick(S, 1024)
    ts_tok = _pick(S, 512)
    rows_tm = 32 * Bn

    row2 = lambda v: v.reshape(1, -1)
    for l in range(depth):
        w_main = jnp.concatenate([w_in[l][:, :c_ab], w_in[l][:, c_ab + n_ab:]], axis=1).astype(BF16)
        w_ab = jnp.pad(w_in[l][:, c_ab:c_ab + n_ab], ((0, 0), (0, LANES - n_ab))).astype(BF16)
        P, ab = _proj(x.reshape(T, D), row2(mix_norm[l]), w_main, w_ab, gdn_conv[l], S, tm_proj)
        P3 = P.reshape(Bn, S, -1)

        prm = jnp.zeros((8, LANES), F32)
        prm = prm.at[0, 2 * H:4 * H].set(gdn_a_log[l].reshape(-1)).at[1, 2 * H:4 * H].set(gdn_dt_bias[l].reshape(-1))
        o_f, o_b = _gdn(P3, ab.reshape(Bn, S, LANES), prm, GDN_HEADS * GDN_CHUNK, 3 * W)

        utm = jnp.swapaxes(P3[:, :, blk["s5_u"] * W:(blk["lru_g"] + 1) * W], 0, 1).reshape(S * Bn, 3 * W)
        bw, cw, a_b = _s5_weights(s5_lambda_re[l], s5_lambda_im[l], s5_log_step[l], s5_b_re[l], s5_b_im[l],
                                  s5_c_re[l], s5_c_im[l], Bn)
        y_f, y_b = _s5(utm, bw, cw, a_b, Bn, rows_tm, 0)
        wg, bg = _lru_gate_weights(lru_gate_a_w[l], lru_gate_x_w[l], lru_gate_a_b[l], lru_gate_x_b[l])
        h_f, h_b = _lru(utm, lru_conv_w[l], row2(lru_conv_b[l]), wg, bg, lru_lambda[l][:, None, :], Bn, rows_tm, 1)
        sl = _post(utm, y_f, y_b, h_f, h_b, row2(s5_d[l]), s5_glu_w[l].astype(BF16), row2(s5_glu_b[l]),
                   rows_tm, 0, 2)
        sl3 = jnp.swapaxes(sl.reshape(S, Bn, 2 * W), 0, 1)

        x = _merge(x, o_f, o_b, P3, sl3, row2(gdn_out_norm[l]), sc_conv[l], w_branch[l].astype(BF16),
                   w_mix_out[l].astype(BF16), ts_tok, blk)

        kv = _kv(mem, row2(xa_mem_norm[l]), xa_w_kv[l].astype(BF16))
        x = _xa(x, kv, row2(xa_norm[l]), xa_w_q[l].astype(BF16), xa_w_o[l].astype(BF16), ts_tok)

        x = _ffn(x, row2(ffn_norm[l]), ffn_w_up[l].astype(BF16), ffn_conv_w[l], row2(ffn_conv_b[l]),
                 ffn_w_down[l].astype(BF16), ts_tok, 256)
    return _final_norm(x.reshape(T, D), row2(final_norm), tm_proj).reshape(Bn, S, D)
```

```python
import functools
import math

import jax
import jax.numpy as jnp
from jax import lax
from jax.experimental import pallas as pl
from jax.experimental.pallas import tpu as pltpu

F32 = jnp.float32
BF16 = jnp.bfloat16
EPS = 1e-6

LANES = 128
BF16_ROWS = 16
VMEM_LIMIT = 56 * 1024 * 1024

GDN_HEADS = 4
GDN_CHUNK = 64
GDN_CONV = 4
GDN_INV_PASSES = 1
LRU_CONV = 4
S5_GROUP = 16
S5_STATE = 64
LRU_C = 8.0
XA_HEADS = 4

HI = lax.Precision.HIGHEST


def _cparams(sem):
    return pltpu.CompilerParams(dimension_semantics=sem, vmem_limit_bytes=VMEM_LIMIT)


def _rms(x, g):
    return x * lax.rsqrt(jnp.mean(x * x, axis=-1, keepdims=True) + EPS) * g


def _dot(a, b):
    return jnp.dot(a, b, preferred_element_type=F32)


def _dot_nt(a, b):
    return lax.dot_general(a, b, (((1,), (1,)), ((), ())), preferred_element_type=F32)


def _sigmoid(x):
    return 0.5 * jnp.tanh(0.5 * x) + 0.5


def _softplus(x):
    return jnp.maximum(x, 0.0) + jnp.log(1.0 + jnp.exp(-jnp.abs(x)))


def _const_spec(shape):
    nd = len(shape)
    return pl.BlockSpec(shape, lambda *_: (0,) * nd)


def _proj_kernel(x_ref, g_ref, w_ref, wab_ref, p_ref, ab_ref, h_ref):
    @pl.when(pl.program_id(1) == 0)
    def _():
        hb = _rms(x_ref[...], g_ref[...]).astype(BF16)
        h_ref[...] = hb
        ab_ref[...] = _dot(hb, wab_ref[...])

    p_ref[...] = _dot(h_ref[...], w_ref[...]).astype(BF16)


def _proj(x2, gain, w_main, w_ab, tm, tn):
    T, D = x2.shape
    N = w_main.shape[1]
    return pl.pallas_call(
        _proj_kernel,
        grid=(T // tm, N // tn),
        in_specs=[
            pl.BlockSpec((tm, D), lambda i, j: (i, 0)),
            pl.BlockSpec((1, D), lambda i, j: (0, 0)),
            pl.BlockSpec((D, tn), lambda i, j: (0, j)),
            pl.BlockSpec((D, LANES), lambda i, j: (0, 0)),
        ],
        out_specs=[
            pl.BlockSpec((tm, tn), lambda i, j: (i, j)),
            pl.BlockSpec((tm, LANES), lambda i, j: (i, 0)),
        ],
        out_shape=[jax.ShapeDtypeStruct((T, N), BF16), jax.ShapeDtypeStruct((T, LANES), F32)],
        scratch_shapes=[pltpu.VMEM((tm, D), BF16)],
        compiler_params=_cparams(("parallel", "arbitrary")),
        name="proj",
    )(x2, gain, w_main, w_ab)


def _qkvprep_kernel(xm_ref, xp_ref, xn_ref, cw_ref, o_ref, *, n_norm):
    i = pl.program_id(1)
    last = pl.num_programs(1) - 1
    ts = xm_ref.shape[1]
    hal = xp_ref.shape[1]
    keep_p = (i > 0).astype(F32)
    keep_n = (i < last).astype(F32)
    row = lax.broadcasted_iota(jnp.int32, (ts, 1), 0)
    for c in range(xm_ref.shape[2] // LANES):
        sl = slice(c * LANES, (c + 1) * LANES)
        x = xm_ref[0, :, sl].astype(F32)
        xp = xp_ref[0, :, sl].astype(F32) * keep_p
        xn = xn_ref[0, :, sl].astype(F32) * keep_n
        m1 = jnp.where(row == 0, xp[hal - 1:hal], pltpu.roll(x, 1, 0))
        m2 = pltpu.roll(x, 2, 0)
        m2 = jnp.where(row == 0, xp[hal - 2:hal - 1], m2)
        m2 = jnp.where(row == 1, xp[hal - 1:hal], m2)
        p1 = jnp.where(row == ts - 1, xn[0:1], pltpu.roll(x, ts - 1, 0))
        w = cw_ref[:, sl]
        y = w[0:1] * m2 + w[1:2] * m1 + w[2:3] * x + w[3:4] * p1
        y = y * _sigmoid(y)
        if c < n_norm:
            y = y * lax.rsqrt(jnp.sum(y * y, axis=-1, keepdims=True) + EPS)
        o_ref[0, :, sl] = y.astype(o_ref.dtype)


def _qkvprep(P3, conv_w, ts):
    Bn, S, _ = P3.shape
    C = conv_w.shape[1]
    hal = BF16_ROWS
    nb = ts // hal
    return pl.pallas_call(
        functools.partial(_qkvprep_kernel, n_norm=2 * GDN_HEADS),
        grid=(Bn, S // ts),
        in_specs=[
            pl.BlockSpec((1, ts, C), lambda b, i: (b, i, 0)),
            pl.BlockSpec((1, hal, C), lambda b, i: (b, jnp.maximum(i * nb - 1, 0), 0)),
            pl.BlockSpec((1, hal, C), lambda b, i: (b, jnp.minimum((i + 1) * nb, S // hal - 1), 0)),
            _const_spec(conv_w.shape),
        ],
        out_specs=pl.BlockSpec((1, ts, C), lambda b, i: (b, i, 0)),
        out_shape=jax.ShapeDtypeStruct((Bn, S, C), BF16),
        compiler_params=_cparams(("parallel", "parallel")),
        name="gdn_prep",
    )(P3, P3, P3, conv_w)


def _mm_split(lhs, w, n_pass):
    lh, wh = lhs.astype(BF16), w.astype(BF16)
    if n_pass == 1:
        return _dot(lh, wh)
    ll = (lhs - lh.astype(F32)).astype(BF16)
    wl = (w - wh.astype(F32)).astype(BF16)
    return _dot(lh, wh) + (_dot(lh, wl) + _dot(ll, wh))


def _gdn_prepare(d, qc, gam, beta, grow, msk):
    C = GDN_CHUNK
    H = GDN_HEADS
    Dh = LANES
    blk, incl, strict, eye = msk
    stack = lambda off: jnp.concatenate([qc[:, off + h * Dh: off + (h + 1) * Dh] for h in range(H)], axis=0)
    Qs = stack(0).astype(F32) * (Dh ** -0.5)
    Kb16 = stack(H * Dh)
    Ks = Kb16.astype(F32)
    Vs = stack(2 * H * Dh).astype(F32)
    gcol = jnp.concatenate(gam, axis=0)
    bcol = jnp.concatenate(beta, axis=0)
    edge = C - 1 if d == 0 else 0
    glast = [g[edge:edge + 1] for g in gam]
    glast_col = jnp.concatenate([jnp.broadcast_to(g, (C, 1)) for g in glast], axis=0)

    diff = _side_by_side(gam, blk) - grow
    decay = jnp.where(incl, jnp.exp(jnp.where(incl, diff, 0.0)), 0.0)
    kb = Ks * bcol
    g2 = _dot_nt(jnp.concatenate([kb, Qs], axis=0).astype(BF16), Kb16)
    kk = _side_by_side([g2[h * C:(h + 1) * C] for h in range(H)], blk)
    qk = _side_by_side([g2[(H + h) * C:(H + h + 1) * C] for h in range(H)], blk) * decay
    X = jnp.where(strict, -(kk * decay), 0.0)
    egam = jnp.exp(gcol)
    return dict(X=X, P=eye + X, qk=qk, rhs=jnp.concatenate([Vs * bcol, kb * egam], axis=1).astype(BF16),
                qdec=Qs * egam, kdec=Ks * jnp.exp(glast_col - gcol), eg=[jnp.exp(g) for g in glast])


def _side_by_side(cols, blk):
    out = jnp.where(blk[0], cols[0], 0.0)
    for h in range(1, len(blk)):
        out = jnp.where(blk[h], cols[h], out)
    return out


def _bd(m, blk):
    return jnp.concatenate([jnp.where(b, m, 0.0) for b in blk], axis=0)


def _gdn_inverse_step(c, k, blk):
    C = GDN_CHUNK
    w = _bd(c["X"], blk)
    if k == 0:
        c["X"] = _mm_split(c["X"], w, GDN_INV_PASSES)
    elif k < 5:
        r = _mm_split(jnp.concatenate([c["P"], c["X"]], axis=0), w, GDN_INV_PASSES)
        c["P"] = c["P"] + r[:C]
        c["X"] = r[C:]
    else:
        c["P"] = c["P"] + _mm_split(c["P"], w, GDN_INV_PASSES)


def _gdn_wy(c, blk):
    C = GDN_CHUNK
    H = GDN_HEADS
    wy = _dot(_bd(c["P"], blk).astype(BF16), c["rhs"])
    c["U"], Wm = wy[:, :LANES], wy[:, LANES:]
    qdec = c["qdec"]
    c["lhs"] = [jnp.concatenate([Wm[h * C:(h + 1) * C], qdec[h * C:(h + 1) * C]], axis=0).astype(BF16)
                for h in range(H)]
    c["kT"] = c["kdec"].T.astype(BF16)
    c["egrow"] = jnp.concatenate([jnp.broadcast_to(e, (1, LANES)) for e in c["eg"]], axis=1)
    c["qkbd"] = _bd(c["qk"], blk).astype(BF16)


def _gdn_kernel(qf_ref, af_ref, qb_ref, abk_ref, prm_ref, of_ref, ob_ref, sf_ref, sb_ref):
    C = GDN_CHUNK
    H = GDN_HEADS
    CT = qf_ref.shape[1]
    nch = CT // C

    @pl.when(pl.program_id(1) == 0)
    def _():
        sf_ref[...] = jnp.zeros_like(sf_ref)
        sb_ref[...] = jnp.zeros_like(sb_ref)

    ri = lax.broadcasted_iota(jnp.int32, (C, H * C), 0)
    lane = lax.broadcasted_iota(jnp.int32, (C, H * C), 1)
    cj = lane % C
    blk = [(lane // C) == h for h in range(H)]
    eye = (cj == ri).astype(F32)
    rt = lax.broadcasted_iota(jnp.int32, (CT, CT), 0)
    ct = lax.broadcasted_iota(jnp.int32, (CT, CT), 1)
    same_t = (rt // C) == (ct // C)
    neg_a = -jnp.exp(prm_ref[0:1, :])
    dt_b = prm_ref[1:2, :]

    dirs = ((0, qf_ref, af_ref, of_ref, sf_ref), (1, qb_ref, abk_ref, ob_ref, sb_ref))

    chunks = {}
    for d, q_ref, a_ref, o_ref, s_ref in dirs:
        ab = a_ref[0]
        bt = _sigmoid(ab)
        la = neg_a * _softplus(ab + dt_b)
        cum = (same_t & ((ct <= rt) if d == 0 else (ct >= rt))).astype(F32)
        gam = jnp.dot(cum, la, precision=HI, preferred_element_type=F32)
        slab = gam.T[2 * H:4 * H, :]
        msk = (blk, (cj <= ri) if d == 0 else (cj >= ri), (cj < ri) if d == 0 else (cj > ri), eye)
        for ci in range(nch):
            rows = slice(ci * C, (ci + 1) * C)
            gl = [gam[rows, 2 * H + d * H + h: 2 * H + d * H + h + 1] for h in range(H)]
            bl = [bt[rows, d * H + h: d * H + h + 1] for h in range(H)]
            pieces = []
            for h in range(H):
                shift = ((h - ci) * C) % CT
                moved = slab if shift == 0 else pltpu.roll(slab, shift, 1)
                pieces.append(moved[d * H + h:d * H + h + 1, :])
            grow = _side_by_side(pieces, [b[0:1, :] for b in blk])
            chunks[d, ci] = _gdn_prepare(d, q_ref[0, rows, :], gl, bl, grow, msk)

    order = [(d, step if d == 0 else nch - 1 - step) for step in range(nch) for d in (0, 1)]
    for k in range(6):
        for key in order:
            _gdn_inverse_step(chunks[key], k, blk)
    for key in order:
        _gdn_wy(chunks[key], blk)

    state = {d: jnp.concatenate([s_ref[h] for h in range(H)], axis=1) for d, _, _, _, s_ref in dirs}
    zero = jnp.zeros((C, LANES), BF16)
    for step in range(nch):
        cur = {d: chunks[d, step if d == 0 else nch - 1 - step] for d in (0, 1)}
        r = {d: [_dot(cur[d]["lhs"][h], state[d][:, h * LANES:(h + 1) * LANES].astype(BF16)) for h in range(H)]
             for d in (0, 1)}
        vbd = {}
        for d, _, _, o_ref, _ in dirs:
            c = cur[d]
            vnew = [(c["U"][h * C:(h + 1) * C] - r[d][h][:C]).astype(BF16) for h in range(H)]
            out = jnp.concatenate([r[d][h][C:] for h in range(H)], axis=0) + _dot(c["qkbd"], jnp.concatenate(vnew, axis=0))
            ci = step if d == 0 else nch - 1 - step
            for h in range(H):
                o_ref[0, ci * C:(ci + 1) * C, h * LANES:(h + 1) * LANES] = out[h * C:(h + 1) * C].astype(o_ref.dtype)
            vbd[d] = jnp.concatenate([jnp.concatenate([vnew[h] if g == h else zero for g in range(H)], axis=1)
                                      for h in range(H)], axis=0)
        for d in (0, 1):
            state[d] = state[d] * cur[d]["egrow"] + _dot(cur[d]["kT"], vbd[d])
    for d, _, _, _, s_ref in dirs:
        for h in range(H):
            s_ref[h] = state[d][:, h * LANES:(h + 1) * LANES]


def _gdn(qkv, ab3, prm, ct, C3):
    Bn, S, _ = qkv.shape
    W = C3 // 3
    nt = S // ct
    dh = W // GDN_HEADS
    assert ct == GDN_HEADS * GDN_CHUNK and dh == LANES
    fwd = lambda b, n: (b, n, 0)
    bwd = lambda b, n: (b, nt - 1 - n, 0)
    return pl.pallas_call(
        _gdn_kernel,
        grid=(Bn, nt),
        in_specs=[
            pl.BlockSpec((1, ct, C3), fwd),
            pl.BlockSpec((1, ct, LANES), fwd),
            pl.BlockSpec((1, ct, C3), bwd),
            pl.BlockSpec((1, ct, LANES), bwd),
            _const_spec(prm.shape),
        ],
        out_specs=[pl.BlockSpec((1, ct, W), fwd), pl.BlockSpec((1, ct, W), bwd)],
        out_shape=[jax.ShapeDtypeStruct((Bn, S, W), BF16)] * 2,
        scratch_shapes=[pltpu.VMEM((GDN_HEADS, dh, dh), F32)] * 2,
        compiler_params=_cparams(("parallel", "arbitrary")),
        name="gdn",
    )(qkv, ab3, qkv, ab3, prm)


def _s5_kernel(uf_ref, ub_ref, bw_ref, cw_ref, a_ref, yf_ref, yb_ref, bu_ref, hb_ref, carry_ref, *, nb):
    R = uf_ref.shape[0]
    tt = R // nb
    nq = bw_ref.shape[1]
    ns = bw_ref.shape[3]
    half = ns // 2
    n_tile = 2 * LANES
    u_refs, y_refs = (uf_ref, ub_ref), (yf_ref, yb_ref)
    units = [(d, q) for d in (0, 1) for q in range(nq)]

    @pl.when(pl.program_id(0) == 0)
    def _():
        carry_ref[...] = jnp.zeros_like(carry_ref)

    def expand(k, n):
        d, q = units[k]
        cols = slice(n * n_tile, (n + 1) * n_tile)
        bu_ref[k % 2, :, cols] = _dot(u_refs[d][:, q * LANES:(q + 1) * LANES], bw_ref[d, q, :, cols])

    def contract(k, part, parts=2):
        d, q = units[k]
        rows = slice(part * R // parts, (part + 1) * R // parts)
        y_refs[d][rows, q * LANES:(q + 1) * LANES] = _dot(hb_ref[k % 2, rows, :], cw_ref[d, q]).astype(y_refs[d].dtype)

    for n in range(ns // n_tile):
        expand(0, n)
    for k, (d, q) in enumerate(units):
        mxu = []
        if k + 1 < len(units):
            mxu += [functools.partial(expand, k + 1, n) for n in range(ns // n_tile)]
        if k > 0:
            mxu += [functools.partial(contract, k - 1, p) for p in range(2)]
        every = -(-tt // len(mxu))
        ar = a_ref[d, 0, :, q * half:(q + 1) * half]
        ai = a_ref[d, 1, :, q * half:(q + 1) * half]
        cre = slice(q * ns, q * ns + half)
        cim = slice(q * ns + half, (q + 1) * ns)
        hr, hi = carry_ref[d, :, cre], carry_ref[d, :, cim]
        for s in range(tt):
            if mxu and s % every == 0:
                mxu.pop(0)()
            t = s if d == 0 else tt - 1 - s
            rows = slice(t * nb, (t + 1) * nb)
            hr, hi = (ar * hr - ai * hi + bu_ref[k % 2, rows, :half],
                      ar * hi + ai * hr + bu_ref[k % 2, rows, half:])
            hb_ref[k % 2, rows, :half] = hr.astype(BF16)
            hb_ref[k % 2, rows, half:] = hi.astype(BF16)
        for piece in mxu:
            piece()
        carry_ref[d, :, cre] = hr
        carry_ref[d, :, cim] = hi
    for p in range(2):
        contract(len(units) - 1, p)


def _s5(utm, bw, cw, a_b, nb, rows, col_blk):
    TB = utm.shape[0]
    nq = bw.shape[1]
    W = nq * LANES
    nstate = bw.shape[3] * nq
    nt = TB // rows
    fwd = lambda n: (n, col_blk)
    bwd = lambda n: (nt - 1 - n, col_blk)
    return pl.pallas_call(
        functools.partial(_s5_kernel, nb=nb),
        grid=(nt,),
        in_specs=[
            pl.BlockSpec((rows, W), fwd),
            pl.BlockSpec((rows, W), bwd),
            _const_spec(bw.shape),
            _const_spec(cw.shape),
            _const_spec(a_b.shape),
        ],
        out_specs=[pl.BlockSpec((rows, W), lambda n: (n, 0)), pl.BlockSpec((rows, W), lambda n: (nt - 1 - n, 0))],
        out_shape=[jax.ShapeDtypeStruct((TB, W), BF16)] * 2,
        scratch_shapes=[pltpu.VMEM((2, rows, bw.shape[3]), F32), pltpu.VMEM((2, rows, bw.shape[3]), BF16),
                        pltpu.VMEM((2, nb, nstate), F32)],
        compiler_params=_cparams(("arbitrary",)),
        name="s5_scan",
    )(utm, utm, bw, cw, a_b)


def _lru_kernel(xf_ref, xfp_ref, xfn_ref, xb_ref, xbp_ref, xbn_ref, cw_ref, cb_ref, wg_ref, bg_ref, lam_ref,
                of_ref, ob_ref, a_s, b_s, carry_ref, *, nb):
    n = pl.program_id(0)
    nt = pl.num_programs(0)
    R = xf_ref.shape[0]
    W = xf_ref.shape[1]
    tt = R // nb

    @pl.when(n == 0)
    def _():
        carry_ref[...] = jnp.zeros_like(carry_ref)

    streams = ((0, n, xf_ref, xfp_ref, xfn_ref, of_ref), (1, nt - 1 - n, xb_ref, xbp_ref, xbn_ref, ob_ref))
    for d, pos, xm_ref, xp_ref, xn_ref, o_ref in streams:
        xp = xp_ref[...].astype(F32) * (pos > 0).astype(F32)
        xn = xn_ref[...].astype(F32) * (pos < nt - 1).astype(F32)
        xe = jnp.concatenate([xp, xm_ref[...].astype(F32), xn], axis=0)
        w = cw_ref[...]
        xc = cb_ref[...] + sum(w[k:k + 1] * xe[k * nb:k * nb + R] for k in range(LRU_CONV))
        th = jnp.tanh(_dot(xc.astype(BF16), wg_ref[d]) + bg_ref[d])
        ig = 0.5 * th[:, W:] + 0.5
        c2 = (-0.5 * LRU_C) * _softplus(-lam_ref[d])
        a = jnp.exp(c2 * th[:, :W] + c2)
        a_s[d] = a
        b_s[d] = jnp.sqrt(1.0 - a * a) * (ig * xc)

        def body(s, h, d=d, o_ref=o_ref):
            t = s if d == 0 else tt - 1 - s
            rows = pl.ds(pl.multiple_of(t * nb, nb), nb)
            h = a_s[d, rows, :] * h + b_s[d, rows, :]
            o_ref[rows, :] = h.astype(o_ref.dtype)
            return h

        carry_ref[d] = lax.fori_loop(0, tt, body, carry_ref[d], unroll=4)


def _lru(utm, conv_w, conv_b, wg, bg, lam, nb, rows, col_blk):
    TB = utm.shape[0]
    W = conv_w.shape[1]
    nt = TB // rows
    hp = 2 * nb
    hn = nb
    nbt = TB // nb

    def specs(pos):
        return [
            pl.BlockSpec((rows, W), lambda n: (pos(n), col_blk)),
            pl.BlockSpec((hp, W), lambda n: (jnp.maximum(pos(n) * (rows // hp) - 1, 0), col_blk)),
            pl.BlockSpec((hn, W), lambda n: (jnp.minimum((pos(n) + 1) * (rows // hn), nbt - 1), col_blk)),
        ]

    fwd = lambda n: n
    bwd = lambda n: nt - 1 - n
    return pl.pallas_call(
        functools.partial(_lru_kernel, nb=nb),
        grid=(nt,),
        in_specs=specs(fwd) + specs(bwd) + [_const_spec(conv_w.shape), _const_spec(conv_b.shape),
                                            _const_spec(wg.shape), _const_spec(bg.shape), _const_spec(lam.shape)],
        out_specs=[pl.BlockSpec((rows, W), lambda n: (n, 0)), pl.BlockSpec((rows, W), lambda n: (nt - 1 - n, 0))],
        out_shape=[jax.ShapeDtypeStruct((TB, W), BF16)] * 2,
        scratch_shapes=[pltpu.VMEM((2, rows, W), F32), pltpu.VMEM((2, rows, W), F32), pltpu.VMEM((2, nb, W), F32)],
        compiler_params=_cparams(("arbitrary",)),
        name="lru_scan",
    )(utm, utm, utm, utm, utm, utm, conv_w, conv_b, wg, bg, lam)


def _post_kernel(u_ref, g_ref, yf_ref, yb_ref, hf_ref, hb_ref, d_ref, gw_ref, gb_ref, o_ref):
    W = u_ref.shape[1]
    y = u_ref[...].astype(F32) * d_ref[...] + yf_ref[...].astype(F32) + yb_ref[...].astype(F32)
    zg = jax.nn.gelu(y)
    o_ref[:, :W] = (zg * _sigmoid(_dot(zg.astype(BF16), gw_ref[...]) + gb_ref[...])).astype(o_ref.dtype)
    h = hf_ref[...].astype(F32) + hb_ref[...].astype(F32)
    o_ref[:, W:] = (h * jax.nn.gelu(g_ref[...].astype(F32))).astype(o_ref.dtype)


def _post(utm, yf, yb, hf, hb, s5_d, glu_w, glu_b, rows, u_blk, g_blk):
    TB, W = yf.shape
    row = lambda n: (n, 0)
    return pl.pallas_call(
        _post_kernel,
        grid=(TB // rows,),
        in_specs=[
            pl.BlockSpec((rows, W), lambda n: (n, u_blk)),
            pl.BlockSpec((rows, W), lambda n: (n, g_blk)),
            pl.BlockSpec((rows, W), row), pl.BlockSpec((rows, W), row),
            pl.BlockSpec((rows, W), row), pl.BlockSpec((rows, W), row),
            _const_spec(s5_d.shape), _const_spec(glu_w.shape), _const_spec(glu_b.shape),
        ],
        out_specs=pl.BlockSpec((rows, 2 * W), row),
        out_shape=jax.ShapeDtypeStruct((TB, 2 * W), BF16),
        compiler_params=_cparams(("parallel",)),
        name="s5_lru_post",
    )(utm, utm, yf, yb, hf, hb, s5_d, glu_w, glu_b)


def _merge_kernel(x_ref, of_ref, ob_ref, z_ref, sl_ref, scb_ref, scc_ref, scx_ref, ccp_ref, cxp_ref, ccn_ref,
                  cxn_ref, g0_ref, g1_ref, g2_ref, g3_ref, gn_ref, scw_ref, wb_ref, wo_ref, o_ref):
    i = pl.program_id(1)
    last = pl.num_programs(1) - 1
    tm = x_ref.shape[1]
    W = z_ref.shape[2]
    hal = ccp_ref.shape[1]

    cx = scc_ref[0].astype(F32) * scx_ref[0].astype(F32)
    cx_p = (ccp_ref[0, hal - 1:hal].astype(F32) * cxp_ref[0, hal - 1:hal].astype(F32)) * (i > 0).astype(F32)
    cx_n = (ccn_ref[0, 0:1].astype(F32) * cxn_ref[0, 0:1].astype(F32)) * (i < last).astype(F32)
    row = lax.broadcasted_iota(jnp.int32, (tm, 1), 0)
    m1 = jnp.where(row == 0, cx_p, pltpu.roll(cx, 1, 0))
    p1 = jnp.where(row == tm - 1, cx_n, pltpu.roll(cx, tm - 1, 0))
    w = scw_ref[...]
    conv = w[0:1] * m1 + w[1:2] * cx + w[2:3] * p1
    g_refs = (g0_ref, g1_ref, g2_ref, g3_ref)

    def gated_sum(rs):
        o = of_ref[0, rs, :].astype(F32) + ob_ref[0, rs, :].astype(F32)
        z = z_ref[0, rs, :].astype(F32)
        ys = []
        for h in range(GDN_HEADS):
            oh = o[:, h * LANES:(h + 1) * LANES]
            ys.append(oh * lax.rsqrt(jnp.mean(oh * oh, axis=-1, keepdims=True) + EPS) * gn_ref[...])
        y_gdn = jnp.concatenate(ys, axis=1) * (z * _sigmoid(z))
        y_sc = scb_ref[0, rs, :].astype(F32) * conv[rs]
        branches = (y_gdn.astype(BF16), sl_ref[0, rs, :W], sl_ref[0, rs, W:], y_sc.astype(BF16))
        merged = None
        for m, y in enumerate(branches):
            hd = _dot(y, wb_ref[m])
            t = hd + hd * jnp.tanh(g_refs[m][0, rs, :].astype(F32))
            merged = t if merged is None else merged + t
        return merged.astype(BF16)

    half = tm // 2
    rows = [slice(0, half), slice(half, tm)]
    merged = gated_sum(rows[0])
    for k in range(2):
        nxt = gated_sum(rows[k + 1]) if k == 0 else None
        o_ref[0, rows[k], :] = x_ref[0, rows[k], :] + _dot(merged, wo_ref[...])
        merged = nxt


def _merge(x3, o_f, o_b, P3, sl3, gdn_gain, sc_w, w_branch, w_out, tm, blk):
    Bn, S, D = x3.shape
    W = o_f.shape[2]
    hal = BF16_ROWS
    nb = tm // hal
    tok = lambda b, i: (b, i, 0)
    col = lambda k: (lambda b, i: (b, i, k))
    prev = lambda k: (lambda b, i: (b, jnp.maximum(i * nb - 1, 0), k))
    nxt = lambda k: (lambda b, i: (b, jnp.minimum((i + 1) * nb, S // hal - 1), k))
    gate0 = blk["gates"] * W // D
    return pl.pallas_call(
        _merge_kernel,
        grid=(Bn, S // tm),
        in_specs=[
            pl.BlockSpec((1, tm, D), tok),
            pl.BlockSpec((1, tm, W), tok), pl.BlockSpec((1, tm, W), tok),
            pl.BlockSpec((1, tm, W), col(blk["z"])),
            pl.BlockSpec((1, tm, 2 * W), tok),
            pl.BlockSpec((1, tm, W), col(blk["sc_b"])),
            pl.BlockSpec((1, tm, W), col(blk["sc_c"])),
            pl.BlockSpec((1, tm, W), col(blk["sc_x"])),
            pl.BlockSpec((1, hal, W), prev(blk["sc_c"])), pl.BlockSpec((1, hal, W), prev(blk["sc_x"])),
            pl.BlockSpec((1, hal, W), nxt(blk["sc_c"])), pl.BlockSpec((1, hal, W), nxt(blk["sc_x"])),
            pl.BlockSpec((1, tm, D), col(gate0)), pl.BlockSpec((1, tm, D), col(gate0 + 1)),
            pl.BlockSpec((1, tm, D), col(gate0 + 2)), pl.BlockSpec((1, tm, D), col(gate0 + 3)),
            _const_spec(gdn_gain.shape), _const_spec(sc_w.shape), _const_spec(w_branch.shape),
            _const_spec(w_out.shape),
        ],
        out_specs=pl.BlockSpec((1, tm, D), tok),
        out_shape=jax.ShapeDtypeStruct((Bn, S, D), F32),
        compiler_params=_cparams(("parallel", "parallel")),
        name="merge",
    )(x3, o_f, o_b, P3, sl3, P3, P3, P3, P3, P3, P3, P3, P3, P3, P3, P3, gdn_gain, sc_w, w_branch, w_out)


def _kv_kernel(m_ref, g_ref, w_ref, o_ref):
    o_ref[0] = _dot(_rms(m_ref[0], g_ref[...]).astype(BF16), w_ref[...]).astype(o_ref.dtype)


def _kv(mem, gain, w_kv):
    Bn, M, D = mem.shape
    return pl.pallas_call(
        _kv_kernel,
        grid=(Bn,),
        in_specs=[pl.BlockSpec((1, M, D), lambda b: (b, 0, 0)), _const_spec(gain.shape), _const_spec(w_kv.shape)],
        out_specs=pl.BlockSpec((1, M, 2 * D), lambda b: (b, 0, 0)),
        out_shape=jax.ShapeDtypeStruct((Bn, M, 2 * D), BF16),
        compiler_params=_cparams(("parallel",)),
        name="xa_kv",
    )(mem, gain, w_kv)


def _xa_kernel(x_ref, kv_ref, g_ref, wq_ref, wo_ref, o_ref):
    D = x_ref.shape[2]
    dh = D // XA_HEADS
    x = x_ref[0]
    q = _dot(_rms(x, g_ref[...]).astype(BF16), wq_ref[...]).astype(BF16)
    outs = []
    for h in range(XA_HEADS):
        k = kv_ref[0, :, h * dh:(h + 1) * dh]
        v = kv_ref[0, :, D + h * dh:D + (h + 1) * dh]
        s = _dot_nt(q[:, h * dh:(h + 1) * dh], k) * (dh ** -0.5)
        e = jnp.exp(s - jnp.max(s, axis=-1, keepdims=True))
        p = e / jnp.sum(e, axis=-1, keepdims=True)
        outs.append(_dot(p.astype(BF16), v))
    o = jnp.concatenate(outs, axis=1).astype(BF16)
    o_ref[0] = x + _dot(o, wo_ref[...])


def _xa(x3, kv, gain, wq, wo, tm):
    Bn, S, D = x3.shape
    M = kv.shape[1]
    tok = lambda b, i: (b, i, 0)
    return pl.pallas_call(
        _xa_kernel,
        grid=(Bn, S // tm),
        in_specs=[pl.BlockSpec((1, tm, D), tok), pl.BlockSpec((1, M, 2 * D), lambda b, i: (b, 0, 0)),
                  _const_spec(gain.shape), _const_spec(wq.shape), _const_spec(wo.shape)],
        out_specs=pl.BlockSpec((1, tm, D), tok),
        out_shape=jax.ShapeDtypeStruct((Bn, S, D), F32),
        compiler_params=_cparams(("parallel", "parallel")),
        name="xattn",
    )(x3, kv, gain, wq, wo)


def _ffn_kernel(x_ref, xp_ref, xn_ref, g_ref, wu_ref, cw_ref, cb_ref, wd_ref, o_ref, act_ref, *, tf):
    i = pl.program_id(1)
    last = pl.num_programs(1) - 1
    tm = x_ref.shape[1]
    hal = xp_ref.shape[1]
    F = wd_ref.shape[0]
    x = x_ref[0]
    g = g_ref[...]
    hp = _rms(xp_ref[0], g) * (i > 0).astype(F32)
    hn = _rms(xn_ref[0], g) * (i < last).astype(F32)
    he = jnp.concatenate([hp, _rms(x, g), hn], axis=0).astype(BF16)
    ext = tm + 2 * hal

    def conv(u, cols):
        w = cw_ref[:, cols]
        m1 = pltpu.roll(u, 1, 0)[hal:hal + tm]
        p1 = pltpu.roll(u, ext - 1, 0)[hal:hal + tm]
        return w[0:1] * m1 + w[1:2] * u[hal:hal + tm] + w[2:3] * p1 + cb_ref[:, cols]

    for f in range(F // tf):
        gc = slice(f * tf, (f + 1) * tf)
        uc = slice(F + f * tf, F + (f + 1) * tf)
        gate = conv(_dot(he, wu_ref[:, gc]), gc)
        up = conv(_dot(he, wu_ref[:, uc]), uc)
        act_ref[:, gc] = (gate * _sigmoid(gate) * up).astype(BF16)
    o_ref[0] = x + _dot(act_ref[...], wd_ref[...])


def _ffn(x3, gain, w_up, conv_w, conv_b, w_down, tm, tf):
    Bn, S, D = x3.shape
    hal = 8
    nb = tm // hal
    tok = lambda b, i: (b, i, 0)
    resident = lambda shape: pl.BlockSpec(shape, lambda b, i: (0, 0), pipeline_mode=pl.Buffered(1))
    return pl.pallas_call(
        functools.partial(_ffn_kernel, tf=tf),
        grid=(Bn, S // tm),
        in_specs=[
            pl.BlockSpec((1, tm, D), tok),
            pl.BlockSpec((1, hal, D), lambda b, i: (b, jnp.maximum(i * nb - 1, 0), 0)),
            pl.BlockSpec((1, hal, D), lambda b, i: (b, jnp.minimum((i + 1) * nb, S // hal - 1), 0)),
            _const_spec(gain.shape), resident(w_up.shape), _const_spec(conv_w.shape), _const_spec(conv_b.shape),
            resident(w_down.shape),
        ],
        out_specs=pl.BlockSpec((1, tm, D), tok),
        out_shape=jax.ShapeDtypeStruct((Bn, S, D), F32),
        scratch_shapes=[pltpu.VMEM((tm, w_down.shape[0]), BF16)],
        compiler_params=_cparams(("parallel", "parallel")),
        name="ffn",
    )(x3, x3, x3, gain, w_up, conv_w, conv_b, w_down)


def _norm_kernel(x_ref, g_ref, o_ref):
    o_ref[...] = _rms(x_ref[...], g_ref[...])


def _final_norm(x2, gain, tm):
    T, D = x2.shape
    return pl.pallas_call(
        _norm_kernel,
        grid=(T // tm,),
        in_specs=[pl.BlockSpec((tm, D), lambda i: (i, 0)), _const_spec(gain.shape)],
        out_specs=pl.BlockSpec((tm, D), lambda i: (i, 0)),
        out_shape=jax.ShapeDtypeStruct((T, D), F32),
        compiler_params=_cparams(("parallel",)),
        name="final_norm",
    )(x2, gain)


def _s5_weights(lam_re, lam_im, log_step, b_re, b_im, c_re, c_im, nb):
    G, P, J = b_re.shape[1:]
    gq = LANES // J
    nq = G // gq
    lr = jnp.minimum(lam_re, -1e-4)
    li = lam_im
    step = jnp.exp(log_step)[..., None]
    mag = jnp.exp(lr * step)
    a_re, a_im = mag * jnp.cos(li * step), mag * jnp.sin(li * step)
    den = lr * lr + li * li
    f_re = ((a_re - 1.0) * lr + a_im * li) / den
    f_im = (a_im * lr - (a_re - 1.0) * li) / den
    bb_re = f_re[..., None] * b_re - f_im[..., None] * b_im
    bb_im = f_re[..., None] * b_im + f_im[..., None] * b_re
    eye = jnp.eye(gq, dtype=F32)

    def expand(bb):
        t = bb.reshape(2, nq, gq, P, J)
        return jnp.einsum("dqgpj,gh->dqgjhp", t, eye).reshape(2, nq, gq * J, gq * P)

    def contract(cc):
        t = cc.reshape(2, nq, gq, J, P)
        return jnp.einsum("dqgjp,gh->dqgphj", t, eye).reshape(2, nq, gq * P, gq * J)

    bw = jnp.concatenate([expand(bb_re), expand(bb_im)], axis=3).astype(BF16)
    cw = jnp.concatenate([contract(c_re), -contract(c_im)], axis=2).astype(BF16)
    a_b = jnp.stack([a_re.reshape(2, G * P), a_im.reshape(2, G * P)], axis=1)
    a_b = jnp.broadcast_to(a_b[:, :, None, :], (2, 2, nb, G * P))
    return bw, cw, a_b


def _lru_gate_weights(wa, wx, ba, bx):
    nblk, bs = wa.shape[1], wa.shape[2]
    eye = jnp.eye(nblk, dtype=F32)
    dense = lambda w: jnp.einsum("dnkm,nl->dnklm", w, eye).reshape(2, nblk * bs, nblk * bs)
    wg = (0.5 * jnp.concatenate([dense(wa), dense(wx)], axis=2)).astype(BF16)
    bg = 0.5 * jnp.concatenate([ba, bx], axis=1)[:, None, :]
    return wg, bg


def _pick(n, pref):
    return pref if n % pref == 0 else n


def kernel(x, mem, mix_norm, w_in, gdn_conv, gdn_a_log, gdn_dt_bias, gdn_out_norm, s5_lambda_re, s5_lambda_im, s5_log_step, s5_b_re, s5_b_im, s5_c_re, s5_c_im, s5_d, s5_glu_w, s5_glu_b, lru_conv_w, lru_conv_b, lru_gate_a_w, lru_gate_a_b, lru_gate_x_w, lru_gate_x_b, lru_lambda, sc_conv, w_branch, w_mix_out, xa_norm, xa_mem_norm, xa_w_q, xa_w_kv, xa_w_o, ffn_norm, ffn_w_up, ffn_conv_w, ffn_conv_b, ffn_w_down, final_norm):
    Bn, S, D = x.shape
    depth = w_in.shape[0]
    W = D // 2
    H = GDN_HEADS
    T = Bn * S
    F = ffn_w_down.shape[1]
    assert Bn % BF16_ROWS == 0 and S % 256 == 0 and W == H * LANES

    blk = {"z": 3, "s5_u": 4, "lru_x": 5, "lru_g": 6, "sc_b": 7, "sc_c": 8, "sc_x": 9, "gates": 10}
    n_ab = 4 * H
    c_ab = 4 * W
    tm_proj = _pick(S, 1024)
    ts_tok = _pick(S, 512)
    rows_tm = 32 * Bn

    row2 = lambda v: v.reshape(1, -1)
    for l in range(depth):
        c_gate = c_ab + n_ab + 6 * W
        w_main = jnp.concatenate([w_in[l][:, :c_ab], w_in[l][:, c_ab + n_ab:c_gate], 0.5 * w_in[l][:, c_gate:]],
                                 axis=1).astype(BF16)
        w_ab = jnp.pad(w_in[l][:, c_ab:c_ab + n_ab], ((0, 0), (0, LANES - n_ab))).astype(BF16)
        P, ab = _proj(x.reshape(T, D), row2(mix_norm[l]), w_main, w_ab, tm_proj, _pick(w_main.shape[1], 2304))
        P3 = P.reshape(Bn, S, -1)

        qkv = _qkvprep(P3, gdn_conv[l], ts_tok)
        prm = jnp.zeros((8, LANES), F32)
        prm = prm.at[0, 2 * H:4 * H].set(gdn_a_log[l].reshape(-1)).at[1, 2 * H:4 * H].set(gdn_dt_bias[l].reshape(-1))
        o_f, o_b = _gdn(qkv, ab.reshape(Bn, S, LANES), prm, GDN_HEADS * GDN_CHUNK, 3 * W)

        utm = jnp.swapaxes(P3[:, :, blk["s5_u"] * W:(blk["lru_g"] + 1) * W], 0, 1).reshape(S * Bn, 3 * W)
        bw, cw, a_b = _s5_weights(s5_lambda_re[l], s5_lambda_im[l], s5_log_step[l], s5_b_re[l], s5_b_im[l],
                                  s5_c_re[l], s5_c_im[l], Bn)
        y_f, y_b = _s5(utm, bw, cw, a_b, Bn, rows_tm, 0)
        wg, bg = _lru_gate_weights(lru_gate_a_w[l], lru_gate_x_w[l], lru_gate_a_b[l], lru_gate_x_b[l])
        h_f, h_b = _lru(utm, lru_conv_w[l], row2(lru_conv_b[l]), wg, bg, lru_lambda[l][:, None, :], Bn, rows_tm, 1)
        sl = _post(utm, y_f, y_b, h_f, h_b, row2(s5_d[l]), s5_glu_w[l].astype(BF16), row2(s5_glu_b[l]),
                   rows_tm, 0, 2)
        sl3 = jnp.swapaxes(sl.reshape(S, Bn, 2 * W), 0, 1)

        x = _merge(x, o_f, o_b, P3, sl3, row2(gdn_out_norm[l]), sc_conv[l], (0.5 * w_branch[l]).astype(BF16),
                   w_mix_out[l].astype(BF16), ts_tok, blk)

        kv = _kv(mem, row2(xa_mem_norm[l]), xa_w_kv[l].astype(BF16))
        x = _xa(x, kv, row2(xa_norm[l]), xa_w_q[l].astype(BF16), xa_w_o[l].astype(BF16), ts_tok)

        x = _ffn(x, row2(ffn_norm[l]), ffn_w_up[l].astype(BF16), ffn_conv_w[l], row2(ffn_conv_b[l]),
                 ffn_w_down[l].astype(BF16), ts_tok, 256)
    return _final_norm(x.reshape(T, D), row2(final_norm), tm_proj).reshape(Bn, S, D)
```

```python
import functools
import math

import jax
import jax.numpy as jnp
from jax import lax
from jax.experimental import pallas as pl
from jax.experimental.pallas import tpu as pltpu

F32 = jnp.float32
BF16 = jnp.bfloat16
EPS = 1e-6

LANES = 128
BF16_ROWS = 16
VMEM_LIMIT = 56 * 1024 * 1024

GDN_HEADS = 4
GDN_CHUNK = 64
GDN_CONV = 4
GDN_INV_PASSES = 1
LRU_CONV = 4
S5_GROUP = 16
S5_STATE = 64
LRU_C = 8.0
XA_HEADS = 4

HI = lax.Precision.HIGHEST


def _cparams(sem):
    return pltpu.CompilerParams(dimension_semantics=sem, vmem_limit_bytes=VMEM_LIMIT)


def _rms(x, g):
    return x * lax.rsqrt(jnp.mean(x * x, axis=-1, keepdims=True) + EPS) * g


def _dot(a, b):
    return jnp.dot(a, b, preferred_element_type=F32)


def _dot_nt(a, b):
    return lax.dot_general(a, b, (((1,), (1,)), ((), ())), preferred_element_type=F32)


def _sigmoid(x):
    return 0.5 * jnp.tanh(0.5 * x) + 0.5


def _softplus(x):
    return jnp.maximum(x, 0.0) + jnp.log(1.0 + jnp.exp(-jnp.abs(x)))


def _const_spec(shape):
    nd = len(shape)
    return pl.BlockSpec(shape, lambda *_: (0,) * nd)


def _proj_kernel(x_ref, g_ref, w_ref, wab_ref, p_ref, ab_ref, h_ref):
    @pl.when(pl.program_id(1) == 0)
    def _():
        hb = _rms(x_ref[...], g_ref[...]).astype(BF16)
        h_ref[...] = hb
        ab_ref[...] = _dot(hb, wab_ref[...])

    p_ref[...] = _dot(h_ref[...], w_ref[...]).astype(BF16)


def _proj(x2, gain, w_main, w_ab, tm, tn):
    T, D = x2.shape
    N = w_main.shape[1]
    return pl.pallas_call(
        _proj_kernel,
        grid=(T // tm, N // tn),
        in_specs=[
            pl.BlockSpec((tm, D), lambda i, j: (i, 0)),
            pl.BlockSpec((1, D), lambda i, j: (0, 0)),
            pl.BlockSpec((D, tn), lambda i, j: (0, j)),
            pl.BlockSpec((D, LANES), lambda i, j: (0, 0)),
        ],
        out_specs=[
            pl.BlockSpec((tm, tn), lambda i, j: (i, j)),
            pl.BlockSpec((tm, LANES), lambda i, j: (i, 0)),
        ],
        out_shape=[jax.ShapeDtypeStruct((T, N), BF16), jax.ShapeDtypeStruct((T, LANES), F32)],
        scratch_shapes=[pltpu.VMEM((tm, D), BF16)],
        compiler_params=_cparams(("parallel", "arbitrary")),
        name="proj",
    )(x2, gain, w_main, w_ab)


def _qkvprep_kernel(xm_ref, xp_ref, xn_ref, cw_ref, o_ref, *, n_norm):
    i = pl.program_id(1)
    last = pl.num_programs(1) - 1
    ts = xm_ref.shape[1]
    hal = xp_ref.shape[1]
    keep_p = (i > 0).astype(F32)
    keep_n = (i < last).astype(F32)
    ext = ts + 2 * hal
    mid = slice(hal, hal + ts)
    for c in range(xm_ref.shape[2] // LANES):
        sl = slice(c * LANES, (c + 1) * LANES)
        x = xm_ref[0, :, sl].astype(F32)
        xe = jnp.concatenate([xp_ref[0, :, sl].astype(F32) * keep_p, x, xn_ref[0, :, sl].astype(F32) * keep_n], axis=0)
        w = cw_ref[:, sl]
        hy = (w[0:1] * pltpu.roll(xe, 2, 0)[mid] + w[1:2] * pltpu.roll(xe, 1, 0)[mid] + w[2:3] * x
              + w[3:4] * pltpu.roll(xe, ext - 1, 0)[mid])
        y = hy + hy * jnp.tanh(hy)
        if c < n_norm:
            y = y * lax.rsqrt(jnp.sum(y * y, axis=-1, keepdims=True) + EPS)
        o_ref[0, :, sl] = y.astype(o_ref.dtype)


def _qkvprep(P3, conv_w, ts):
    Bn, S, _ = P3.shape
    C = conv_w.shape[1]
    hal = BF16_ROWS
    nb = ts // hal
    return pl.pallas_call(
        functools.partial(_qkvprep_kernel, n_norm=2 * GDN_HEADS),
        grid=(Bn, S // ts),
        in_specs=[
            pl.BlockSpec((1, ts, C), lambda b, i: (b, i, 0)),
            pl.BlockSpec((1, hal, C), lambda b, i: (b, jnp.maximum(i * nb - 1, 0), 0)),
            pl.BlockSpec((1, hal, C), lambda b, i: (b, jnp.minimum((i + 1) * nb, S // hal - 1), 0)),
            _const_spec(conv_w.shape),
        ],
        out_specs=pl.BlockSpec((1, ts, C), lambda b, i: (b, i, 0)),
        out_shape=jax.ShapeDtypeStruct((Bn, S, C), BF16),
        compiler_params=_cparams(("parallel", "parallel")),
        name="gdn_prep",
    )(P3, P3, P3, conv_w)


def _mm_split(lhs, w, n_pass):
    lh, wh = lhs.astype(BF16), w.astype(BF16)
    if n_pass == 1:
        return _dot(lh, wh)
    ll = (lhs - lh.astype(F32)).astype(BF16)
    wl = (w - wh.astype(F32)).astype(BF16)
    return _dot(lh, wh) + (_dot(lh, wl) + _dot(ll, wh))


def _gdn_prepare(d, qc, gam, beta, grow, msk):
    C = GDN_CHUNK
    H = GDN_HEADS
    Dh = LANES
    blk, incl, strict, eye = msk
    stack = lambda off: jnp.concatenate([qc[:, off + h * Dh: off + (h + 1) * Dh] for h in range(H)], axis=0)
    Qs = stack(0).astype(F32) * (Dh ** -0.5)
    Kb16 = stack(H * Dh)
    Ks = Kb16.astype(F32)
    Vs = stack(2 * H * Dh).astype(F32)
    gcol = jnp.concatenate(gam, axis=0)
    bcol = jnp.concatenate(beta, axis=0)
    edge = C - 1 if d == 0 else 0
    glast = [g[edge:edge + 1] for g in gam]
    glast_col = jnp.concatenate([jnp.broadcast_to(g, (C, 1)) for g in glast], axis=0)

    diff = _side_by_side(gam, blk) - grow
    decay = jnp.where(incl, jnp.exp(jnp.where(incl, diff, 0.0)), 0.0)
    kb = Ks * bcol
    g2 = _dot_nt(jnp.concatenate([kb, Qs], axis=0).astype(BF16), Kb16)
    kk = _side_by_side([g2[h * C:(h + 1) * C] for h in range(H)], blk)
    qk = _side_by_side([g2[(H + h) * C:(H + h + 1) * C] for h in range(H)], blk) * decay
    X = jnp.where(strict, -(kk * decay), 0.0)
    egam = jnp.exp(gcol)
    return dict(X=X, P=eye + X, qk=qk, rhs=jnp.concatenate([Vs * bcol, kb * egam], axis=1).astype(BF16),
                qdec=Qs * egam, kdec=Ks * jnp.exp(glast_col - gcol), eg=[jnp.exp(g) for g in glast])


def _side_by_side(cols, blk):
    out = jnp.where(blk[0], cols[0], 0.0)
    for h in range(1, len(blk)):
        out = jnp.where(blk[h], cols[h], out)
    return out


def _bd(m, blk):
    return jnp.concatenate([jnp.where(b, m, 0.0) for b in blk], axis=0)


def _gdn_inverse_step(c, k, blk):
    C = GDN_CHUNK
    w = _bd(c["X"], blk)
    if k == 0:
        c["X"] = _mm_split(c["X"], w, GDN_INV_PASSES)
    elif k < 5:
        r = _mm_split(jnp.concatenate([c["P"], c["X"]], axis=0), w, GDN_INV_PASSES)
        c["P"] = c["P"] + r[:C]
        c["X"] = r[C:]
    else:
        c["P"] = c["P"] + _mm_split(c["P"], w, GDN_INV_PASSES)


def _gdn_wy(c, blk):
    C = GDN_CHUNK
    H = GDN_HEADS
    wy = _dot(_bd(c["P"], blk).astype(BF16), c["rhs"])
    c["U"], Wm = wy[:, :LANES], wy[:, LANES:]
    qdec = c["qdec"]
    c["lhs"] = [jnp.concatenate([Wm[h * C:(h + 1) * C], qdec[h * C:(h + 1) * C]], axis=0).astype(BF16)
                for h in range(H)]
    c["kT"] = c["kdec"].T.astype(BF16)
    c["egrow"] = jnp.concatenate([jnp.broadcast_to(e, (1, LANES)) for e in c["eg"]], axis=1)
    c["qkbd"] = _bd(c["qk"], blk).astype(BF16)


def _gdn_kernel(qf_ref, af_ref, qb_ref, abk_ref, prm_ref, of_ref, ob_ref, sf_ref, sb_ref):
    C = GDN_CHUNK
    H = GDN_HEADS
    CT = qf_ref.shape[1]
    nch = CT // C

    @pl.when(pl.program_id(1) == 0)
    def _():
        sf_ref[...] = jnp.zeros_like(sf_ref)
        sb_ref[...] = jnp.zeros_like(sb_ref)

    ri = lax.broadcasted_iota(jnp.int32, (C, H * C), 0)
    lane = lax.broadcasted_iota(jnp.int32, (C, H * C), 1)
    cj = lane % C
    blk = [(lane // C) == h for h in range(H)]
    eye = (cj == ri).astype(F32)
    rt = lax.broadcasted_iota(jnp.int32, (CT, CT), 0)
    ct = lax.broadcasted_iota(jnp.int32, (CT, CT), 1)
    same_t = (rt // C) == (ct // C)
    neg_a = -jnp.exp(prm_ref[0:1, :])
    dt_b = prm_ref[1:2, :]

    dirs = ((0, qf_ref, af_ref, of_ref, sf_ref), (1, qb_ref, abk_ref, ob_ref, sb_ref))

    chunks = {}
    for d, q_ref, a_ref, o_ref, s_ref in dirs:
        ab = a_ref[0]
        bt = _sigmoid(ab)
        la = neg_a * _softplus(ab + dt_b)
        cum = (same_t & ((ct <= rt) if d == 0 else (ct >= rt))).astype(F32)
        gam = jnp.dot(cum, la, precision=HI, preferred_element_type=F32)
        slab = gam.T[2 * H:4 * H, :]
        msk = (blk, (cj <= ri) if d == 0 else (cj >= ri), (cj < ri) if d == 0 else (cj > ri), eye)
        for ci in range(nch):
            rows = slice(ci * C, (ci + 1) * C)
            gl = [gam[rows, 2 * H + d * H + h: 2 * H + d * H + h + 1] for h in range(H)]
            bl = [bt[rows, d * H + h: d * H + h + 1] for h in range(H)]
            pieces = []
            for h in range(H):
                shift = ((h - ci) * C) % CT
                moved = slab if shift == 0 else pltpu.roll(slab, shift, 1)
                pieces.append(moved[d * H + h:d * H + h + 1, :])
            grow = _side_by_side(pieces, [b[0:1, :] for b in blk])
            chunks[d, ci] = _gdn_prepare(d, q_ref[0, rows, :], gl, bl, grow, msk)

    order = [(d, step if d == 0 else nch - 1 - step) for step in range(nch) for d in (0, 1)]
    for k in range(6):
        for key in order:
            _gdn_inverse_step(chunks[key], k, blk)
    for key in order:
        _gdn_wy(chunks[key], blk)

    state = {d: jnp.concatenate([s_ref[h] for h in range(H)], axis=1) for d, _, _, _, s_ref in dirs}
    zero = jnp.zeros((C, LANES), BF16)
    for step in range(nch):
        cur = {d: chunks[d, step if d == 0 else nch - 1 - step] for d in (0, 1)}
        r = {d: [_dot(cur[d]["lhs"][h], state[d][:, h * LANES:(h + 1) * LANES].astype(BF16)) for h in range(H)]
             for d in (0, 1)}
        vbd = {}
        for d, _, _, o_ref, _ in dirs:
            c = cur[d]
            vnew = [(c["U"][h * C:(h + 1) * C] - r[d][h][:C]).astype(BF16) for h in range(H)]
            out = jnp.concatenate([r[d][h][C:] for h in range(H)], axis=0) + _dot(c["qkbd"], jnp.concatenate(vnew, axis=0))
            ci = step if d == 0 else nch - 1 - step
            for h in range(H):
                o_ref[0, ci * C:(ci + 1) * C, h * LANES:(h + 1) * LANES] = out[h * C:(h + 1) * C].astype(o_ref.dtype)
            vbd[d] = jnp.concatenate([jnp.concatenate([vnew[h] if g == h else zero for g in range(H)], axis=1)
                                      for h in range(H)], axis=0)
        for d in (0, 1):
            state[d] = state[d] * cur[d]["egrow"] + _dot(cur[d]["kT"], vbd[d])
    for d, _, _, _, s_ref in dirs:
        for h in range(H):
            s_ref[h] = state[d][:, h * LANES:(h + 1) * LANES]


def _gdn(qkv, ab3, prm, ct, C3):
    Bn, S, _ = qkv.shape
    W = C3 // 3
    nt = S // ct
    dh = W // GDN_HEADS
    assert ct == GDN_HEADS * GDN_CHUNK and dh == LANES
    fwd = lambda b, n: (b, n, 0)
    bwd = lambda b, n: (b, nt - 1 - n, 0)
    return pl.pallas_call(
        _gdn_kernel,
        grid=(Bn, nt),
        in_specs=[
            pl.BlockSpec((1, ct, C3), fwd),
            pl.BlockSpec((1, ct, LANES), fwd),
            pl.BlockSpec((1, ct, C3), bwd),
            pl.BlockSpec((1, ct, LANES), bwd),
            _const_spec(prm.shape),
        ],
        out_specs=[pl.BlockSpec((1, ct, W), fwd), pl.BlockSpec((1, ct, W), bwd)],
        out_shape=[jax.ShapeDtypeStruct((Bn, S, W), BF16)] * 2,
        scratch_shapes=[pltpu.VMEM((GDN_HEADS, dh, dh), F32)] * 2,
        compiler_params=_cparams(("parallel", "arbitrary")),
        name="gdn",
    )(qkv, ab3, qkv, ab3, prm)


def _s5_kernel(uf_ref, ub_ref, bw_ref, cw_ref, a_ref, yf_ref, yb_ref, bu_ref, hb_ref, carry_ref, *, nb):
    R = uf_ref.shape[0]
    tt = R // nb
    nq = bw_ref.shape[1]
    ns = bw_ref.shape[3]
    half = ns // 2
    n_tile = 2 * LANES
    u_refs, y_refs = (uf_ref, ub_ref), (yf_ref, yb_ref)
    units = [(d, q) for d in (0, 1) for q in range(nq)]

    @pl.when(pl.program_id(0) == 0)
    def _():
        carry_ref[...] = jnp.zeros_like(carry_ref)

    def expand(k, n):
        d, q = units[k]
        cols = slice(n * n_tile, (n + 1) * n_tile)
        bu_ref[k % 2, :, cols] = _dot(u_refs[d][:, q * LANES:(q + 1) * LANES], bw_ref[d, q, :, cols])

    def contract(k, part, parts=2):
        d, q = units[k]
        rows = slice(part * R // parts, (part + 1) * R // parts)
        y_refs[d][rows, q * LANES:(q + 1) * LANES] = _dot(hb_ref[k % 2, rows, :], cw_ref[d, q]).astype(y_refs[d].dtype)

    for n in range(ns // n_tile):
        expand(0, n)
    for k, (d, q) in enumerate(units):
        mxu = []
        if k + 1 < len(units):
            mxu += [functools.partial(expand, k + 1, n) for n in range(ns // n_tile)]
        if k > 0:
            mxu += [functools.partial(contract, k - 1, p) for p in range(2)]
        every = -(-tt // len(mxu))
        ar = a_ref[d, 0, :, q * half:(q + 1) * half]
        ai = a_ref[d, 1, :, q * half:(q + 1) * half]
        cre = slice(q * ns, q * ns + half)
        cim = slice(q * ns + half, (q + 1) * ns)
        hr, hi = carry_ref[d, :, cre], carry_ref[d, :, cim]
        for s in range(tt):
            if mxu and s % every == 0:
                mxu.pop(0)()
            t = s if d == 0 else tt - 1 - s
            rows = slice(t * nb, (t + 1) * nb)
            hr, hi = (ar * hr - ai * hi + bu_ref[k % 2, rows, :half],
                      ar * hi + ai * hr + bu_ref[k % 2, rows, half:])
            hb_ref[k % 2, rows, :half] = hr.astype(BF16)
            hb_ref[k % 2, rows, half:] = hi.astype(BF16)
        for piece in mxu:
            piece()
        carry_ref[d, :, cre] = hr
        carry_ref[d, :, cim] = hi
    for p in range(2):
        contract(len(units) - 1, p)


def _s5(utm, bw, cw, a_b, nb, rows, col_blk):
    TB = utm.shape[0]
    nq = bw.shape[1]
    W = nq * LANES
    nstate = bw.shape[3] * nq
    nt = TB // rows
    fwd = lambda n: (n, col_blk)
    bwd = lambda n: (nt - 1 - n, col_blk)
    return pl.pallas_call(
        functools.partial(_s5_kernel, nb=nb),
        grid=(nt,),
        in_specs=[
            pl.BlockSpec((rows, W), fwd),
            pl.BlockSpec((rows, W), bwd),
            _const_spec(bw.shape),
            _const_spec(cw.shape),
            _const_spec(a_b.shape),
        ],
        out_specs=[pl.BlockSpec((rows, W), lambda n: (n, 0)), pl.BlockSpec((rows, W), lambda n: (nt - 1 - n, 0))],
        out_shape=[jax.ShapeDtypeStruct((TB, W), BF16)] * 2,
        scratch_shapes=[pltpu.VMEM((2, rows, bw.shape[3]), F32), pltpu.VMEM((2, rows, bw.shape[3]), BF16),
                        pltpu.VMEM((2, nb, nstate), F32)],
        compiler_params=_cparams(("arbitrary",)),
        name="s5_scan",
    )(utm, utm, bw, cw, a_b)


def _lru_kernel(xf_ref, xfp_ref, xfn_ref, xb_ref, xbp_ref, xbn_ref, cw_ref, cb_ref, wg_ref, bg_ref, lam_ref,
                of_ref, ob_ref, a_s, b_s, carry_ref, *, nb):
    n = pl.program_id(0)
    nt = pl.num_programs(0)
    R = xf_ref.shape[0]
    W = xf_ref.shape[1]
    tt = R // nb

    @pl.when(n == 0)
    def _():
        carry_ref[...] = jnp.zeros_like(carry_ref)

    streams = ((0, n, xf_ref, xfp_ref, xfn_ref, of_ref), (1, nt - 1 - n, xb_ref, xbp_ref, xbn_ref, ob_ref))
    for d, pos, xm_ref, xp_ref, xn_ref, o_ref in streams:
        xp = xp_ref[...].astype(F32) * (pos > 0).astype(F32)
        xn = xn_ref[...].astype(F32) * (pos < nt - 1).astype(F32)
        xe = jnp.concatenate([xp, xm_ref[...].astype(F32), xn], axis=0)
        w = cw_ref[...]
        xc = cb_ref[...] + sum(w[k:k + 1] * xe[k * nb:k * nb + R] for k in range(LRU_CONV))
        th = jnp.tanh(_dot(xc.astype(BF16), wg_ref[d]) + bg_ref[d])
        ig = 0.5 * th[:, W:] + 0.5
        c2 = (-0.5 * LRU_C) * _softplus(-lam_ref[d])
        a = jnp.exp(c2 * th[:, :W] + c2)
        a_s[d] = a
        b_s[d] = jnp.sqrt(1.0 - a * a) * (ig * xc)

        def body(s, h, d=d, o_ref=o_ref):
            t = s if d == 0 else tt - 1 - s
            rows = pl.ds(pl.multiple_of(t * nb, nb), nb)
            h = a_s[d, rows, :] * h + b_s[d, rows, :]
            o_ref[rows, :] = h.astype(o_ref.dtype)
            return h

        carry_ref[d] = lax.fori_loop(0, tt, body, carry_ref[d], unroll=4)


def _lru(utm, conv_w, conv_b, wg, bg, lam, nb, rows, col_blk):
    TB = utm.shape[0]
    W = conv_w.shape[1]
    nt = TB // rows
    hp = 2 * nb
    hn = nb
    nbt = TB // nb

    def specs(pos):
        return [
            pl.BlockSpec((rows, W), lambda n: (pos(n), col_blk)),
            pl.BlockSpec((hp, W), lambda n: (jnp.maximum(pos(n) * (rows // hp) - 1, 0), col_blk)),
            pl.BlockSpec((hn, W), lambda n: (jnp.minimum((pos(n) + 1) * (rows // hn), nbt - 1), col_blk)),
        ]

    fwd = lambda n: n
    bwd = lambda n: nt - 1 - n
    return pl.pallas_call(
        functools.partial(_lru_kernel, nb=nb),
        grid=(nt,),
        in_specs=specs(fwd) + specs(bwd) + [_const_spec(conv_w.shape), _const_spec(conv_b.shape),
                                            _const_spec(wg.shape), _const_spec(bg.shape), _const_spec(lam.shape)],
        out_specs=[pl.BlockSpec((rows, W), lambda n: (n, 0)), pl.BlockSpec((rows, W), lambda n: (nt - 1 - n, 0))],
        out_shape=[jax.ShapeDtypeStruct((TB, W), BF16)] * 2,
        scratch_shapes=[pltpu.VMEM((2, rows, W), F32), pltpu.VMEM((2, rows, W), F32), pltpu.VMEM((2, nb, W), F32)],
        compiler_params=_cparams(("arbitrary",)),
        name="lru_scan",
    )(utm, utm, utm, utm, utm, utm, conv_w, conv_b, wg, bg, lam)


def _post_kernel(u_ref, g_ref, yf_ref, yb_ref, hf_ref, hb_ref, d_ref, gw_ref, gb_ref, perm_ref, o_ref):
    nb, tt = o_ref.shape[0], o_ref.shape[1]
    y = u_ref[...].astype(F32) * d_ref[...] + yf_ref[...].astype(F32) + yb_ref[...].astype(F32)
    zg = jax.nn.gelu(y)
    s5 = zg * _sigmoid(_dot(zg.astype(BF16), gw_ref[...]) + gb_ref[...])
    h = hf_ref[...].astype(F32) + hb_ref[...].astype(F32)
    lru = h * jax.nn.gelu(g_ref[...].astype(F32))
    res = jnp.concatenate([s5, lru], axis=1).astype(BF16)
    tok = _dot(perm_ref[...], res).astype(o_ref.dtype)
    for b in range(nb):
        o_ref[b] = tok[b * tt:(b + 1) * tt]


def _post(utm, yf, yb, hf, hb, s5_d, glu_w, glu_b, nb, rows, u_blk, g_blk):
    TB, W = yf.shape
    tt = rows // nb
    r = jnp.arange(rows)
    perm = (r[None, :] == ((r % tt) * nb + r // tt)[:, None]).astype(BF16)
    row = lambda n: (n, 0)
    return pl.pallas_call(
        _post_kernel,
        grid=(TB // rows,),
        in_specs=[
            pl.BlockSpec((rows, W), lambda n: (n, u_blk)),
            pl.BlockSpec((rows, W), lambda n: (n, g_blk)),
            pl.BlockSpec((rows, W), row), pl.BlockSpec((rows, W), row),
            pl.BlockSpec((rows, W), row), pl.BlockSpec((rows, W), row),
            _const_spec(s5_d.shape), _const_spec(glu_w.shape), _const_spec(glu_b.shape), _const_spec(perm.shape),
        ],
        out_specs=pl.BlockSpec((nb, tt, 2 * W), lambda n: (0, n, 0)),
        out_shape=jax.ShapeDtypeStruct((nb, TB // nb, 2 * W), BF16),
        compiler_params=_cparams(("parallel",)),
        name="s5_lru_post",
    )(utm, utm, yf, yb, hf, hb, s5_d, glu_w, glu_b, perm)


def _merge_kernel(x_ref, of_ref, ob_ref, z_ref, sl_ref, scb_ref, scc_ref, scx_ref, ccp_ref, cxp_ref, ccn_ref,
                  cxn_ref, g0_ref, g1_ref, g2_ref, g3_ref, gn_ref, scw_ref, wb_ref, wo_ref, o_ref):
    i = pl.program_id(1)
    last = pl.num_programs(1) - 1
    tm = x_ref.shape[1]
    W = z_ref.shape[2]
    hal = ccp_ref.shape[1]

    cx = scc_ref[0].astype(F32) * scx_ref[0].astype(F32)
    cx_p = (ccp_ref[0, hal - 1:hal].astype(F32) * cxp_ref[0, hal - 1:hal].astype(F32)) * (i > 0).astype(F32)
    cx_n = (ccn_ref[0, 0:1].astype(F32) * cxn_ref[0, 0:1].astype(F32)) * (i < last).astype(F32)
    row = lax.broadcasted_iota(jnp.int32, (tm, 1), 0)
    m1 = jnp.where(row == 0, cx_p, pltpu.roll(cx, 1, 0))
    p1 = jnp.where(row == tm - 1, cx_n, pltpu.roll(cx, tm - 1, 0))
    w = scw_ref[...]
    conv = w[0:1] * m1 + w[1:2] * cx + w[2:3] * p1
    g_refs = (g0_ref, g1_ref, g2_ref, g3_ref)

    def gated_sum(rs):
        o = of_ref[0, rs, :].astype(F32) + ob_ref[0, rs, :].astype(F32)
        z = z_ref[0, rs, :].astype(F32)
        ys = []
        for h in range(GDN_HEADS):
            oh = o[:, h * LANES:(h + 1) * LANES]
            ys.append(oh * lax.rsqrt(jnp.mean(oh * oh, axis=-1, keepdims=True) + EPS) * gn_ref[...])
        y_gdn = jnp.concatenate(ys, axis=1) * (z * _sigmoid(z))
        y_sc = scb_ref[0, rs, :].astype(F32) * conv[rs]
        branches = (y_gdn.astype(BF16), sl_ref[0, rs, :W], sl_ref[0, rs, W:], y_sc.astype(BF16))
        merged = None
        for m, y in enumerate(branches):
            hd = _dot(y, wb_ref[m])
            t = hd + hd * jnp.tanh(g_refs[m][0, rs, :].astype(F32))
            merged = t if merged is None else merged + t
        return merged.astype(BF16)

    half = tm // 2
    rows = [slice(0, half), slice(half, tm)]
    merged = gated_sum(rows[0])
    for k in range(2):
        nxt = gated_sum(rows[k + 1]) if k == 0 else None
        o_ref[0, rows[k], :] = x_ref[0, rows[k], :] + _dot(merged, wo_ref[...])
        merged = nxt


def _merge(x3, o_f, o_b, P3, sl3, gdn_gain, sc_w, w_branch, w_out, tm, blk):
    Bn, S, D = x3.shape
    W = o_f.shape[2]
    hal = BF16_ROWS
    nb = tm // hal
    tok = lambda b, i: (b, i, 0)
    col = lambda k: (lambda b, i: (b, i, k))
    prev = lambda k: (lambda b, i: (b, jnp.maximum(i * nb - 1, 0), k))
    nxt = lambda k: (lambda b, i: (b, jnp.minimum((i + 1) * nb, S // hal - 1), k))
    gate0 = blk["gates"] * W // D
    return pl.pallas_call(
        _merge_kernel,
        grid=(Bn, S // tm),
        in_specs=[
            pl.BlockSpec((1, tm, D), tok),
            pl.BlockSpec((1, tm, W), tok), pl.BlockSpec((1, tm, W), tok),
            pl.BlockSpec((1, tm, W), col(blk["z"])),
            pl.BlockSpec((1, tm, 2 * W), tok),
            pl.BlockSpec((1, tm, W), col(blk["sc_b"])),
            pl.BlockSpec((1, tm, W), col(blk["sc_c"])),
            pl.BlockSpec((1, tm, W), col(blk["sc_x"])),
            pl.BlockSpec((1, hal, W), prev(blk["sc_c"])), pl.BlockSpec((1, hal, W), prev(blk["sc_x"])),
            pl.BlockSpec((1, hal, W), nxt(blk["sc_c"])), pl.BlockSpec((1, hal, W), nxt(blk["sc_x"])),
            pl.BlockSpec((1, tm, D), col(gate0)), pl.BlockSpec((1, tm, D), col(gate0 + 1)),
            pl.BlockSpec((1, tm, D), col(gate0 + 2)), pl.BlockSpec((1, tm, D), col(gate0 + 3)),
            _const_spec(gdn_gain.shape), _const_spec(sc_w.shape), _const_spec(w_branch.shape),
            _const_spec(w_out.shape),
        ],
        out_specs=pl.BlockSpec((1, tm, D), tok),
        out_shape=jax.ShapeDtypeStruct((Bn, S, D), F32),
        compiler_params=_cparams(("parallel", "parallel")),
        name="merge",
    )(x3, o_f, o_b, P3, sl3, P3, P3, P3, P3, P3, P3, P3, P3, P3, P3, P3, gdn_gain, sc_w, w_branch, w_out)


def _kv_kernel(m_ref, g_ref, w_ref, o_ref):
    o_ref[0] = _dot(_rms(m_ref[0], g_ref[...]).astype(BF16), w_ref[...]).astype(o_ref.dtype)


def _kv(mem, gain, w_kv):
    Bn, M, D = mem.shape
    return pl.pallas_call(
        _kv_kernel,
        grid=(Bn,),
        in_specs=[pl.BlockSpec((1, M, D), lambda b: (b, 0, 0)), _const_spec(gain.shape), _const_spec(w_kv.shape)],
        out_specs=pl.BlockSpec((1, M, 2 * D), lambda b: (b, 0, 0)),
        out_shape=jax.ShapeDtypeStruct((Bn, M, 2 * D), BF16),
        compiler_params=_cparams(("parallel",)),
        name="xa_kv",
    )(mem, gain, w_kv)


def _xa_kernel(x_ref, kv_ref, g_ref, wq_ref, wo_ref, o_ref):
    D = x_ref.shape[2]
    dh = D // XA_HEADS
    x = x_ref[0]
    q = _dot(_rms(x, g_ref[...]).astype(BF16), wq_ref[...]).astype(BF16)
    def scores(h):
        return _dot_nt(q[:, h * dh:(h + 1) * dh], kv_ref[0, :, h * dh:(h + 1) * dh]) * (dh ** -0.5)

    def softmax(s):
        e = jnp.exp(s - jnp.max(s, axis=-1, keepdims=True))
        return (e / jnp.sum(e, axis=-1, keepdims=True)).astype(BF16)

    def values(h, p):
        return _dot(p, kv_ref[0, :, D + h * dh:D + (h + 1) * dh])

    H = XA_HEADS
    s, p, outs = {}, {}, {}
    s[0] = scores(0)
    for h in range(H + 2):
        if h + 1 < H:
            s[h + 1] = scores(h + 1)
        if h < H:
            p[h] = softmax(s[h])
        if 1 <= h <= H:
            outs[h - 1] = values(h - 1, p[h - 1])
    o = jnp.concatenate([outs[h] for h in range(H)], axis=1).astype(BF16)
    o_ref[0] = x + _dot(o, wo_ref[...])


def _xa(x3, kv, gain, wq, wo, tm):
    Bn, S, D = x3.shape
    M = kv.shape[1]
    tok = lambda b, i: (b, i, 0)
    return pl.pallas_call(
        _xa_kernel,
        grid=(Bn, S // tm),
        in_specs=[pl.BlockSpec((1, tm, D), tok), pl.BlockSpec((1, M, 2 * D), lambda b, i: (b, 0, 0)),
                  _const_spec(gain.shape), _const_spec(wq.shape), _const_spec(wo.shape)],
        out_specs=pl.BlockSpec((1, tm, D), tok),
        out_shape=jax.ShapeDtypeStruct((Bn, S, D), F32),
        compiler_params=_cparams(("parallel", "parallel")),
        name="xattn",
    )(x3, kv, gain, wq, wo)


def _ffn_kernel(x_ref, xp_ref, xn_ref, g_ref, wu_ref, cw_ref, cb_ref, wd_ref, *rest, tf):
    fg_ref = rest[0] if len(rest) == 3 else None
    o_ref, act_ref = rest[-2:]
    i = pl.program_id(1)
    last = pl.num_programs(1) - 1
    tm = x_ref.shape[1]
    hal = xp_ref.shape[1]
    F = wd_ref.shape[0]
    x = x_ref[0]
    g = g_ref[...]
    hp = _rms(xp_ref[0], g) * (i > 0).astype(F32)
    hn = _rms(xn_ref[0], g) * (i < last).astype(F32)
    he = jnp.concatenate([hp, _rms(x, g), hn], axis=0).astype(BF16)
    ext = tm + 2 * hal

    def conv(u, cols):
        w = cw_ref[:, cols]
        m1 = pltpu.roll(u, 1, 0)[hal:hal + tm]
        p1 = pltpu.roll(u, ext - 1, 0)[hal:hal + tm]
        return w[0:1] * m1 + w[1:2] * u[hal:hal + tm] + w[2:3] * p1 + cb_ref[:, cols]

    for f in range(F // tf):
        gc = slice(f * tf, (f + 1) * tf)
        uc = slice(F + f * tf, F + (f + 1) * tf)
        gate = conv(_dot(he, wu_ref[:, gc]), gc)
        up = conv(_dot(he, wu_ref[:, uc]), uc)
        act_ref[:, gc] = (gate * _sigmoid(gate) * up).astype(BF16)
    out = x + _dot(act_ref[...], wd_ref[...])
    o_ref[0] = out if fg_ref is None else _rms(out, fg_ref[...])


def _ffn(x3, gain, w_up, conv_w, conv_b, w_down, tm, tf, final_gain=None):
    Bn, S, D = x3.shape
    hal = 8
    nb = tm // hal
    tok = lambda b, i: (b, i, 0)
    resident = lambda shape: pl.BlockSpec(shape, lambda b, i: (0, 0), pipeline_mode=pl.Buffered(1))
    return pl.pallas_call(
        functools.partial(_ffn_kernel, tf=tf),
        grid=(Bn, S // tm),
        in_specs=[
            pl.BlockSpec((1, tm, D), tok),
            pl.BlockSpec((1, hal, D), lambda b, i: (b, jnp.maximum(i * nb - 1, 0), 0)),
            pl.BlockSpec((1, hal, D), lambda b, i: (b, jnp.minimum((i + 1) * nb, S // hal - 1), 0)),
            _const_spec(gain.shape), resident(w_up.shape), _const_spec(conv_w.shape), _const_spec(conv_b.shape),
            resident(w_down.shape),
        ] + ([] if final_gain is None else [_const_spec(final_gain.shape)]),
        out_specs=pl.BlockSpec((1, tm, D), tok),
        out_shape=jax.ShapeDtypeStruct((Bn, S, D), F32),
        scratch_shapes=[pltpu.VMEM((tm, w_down.shape[0]), BF16)],
        compiler_params=_cparams(("parallel", "parallel")),
        name="ffn",
    )(x3, x3, x3, gain, w_up, conv_w, conv_b, w_down, *([] if final_gain is None else [final_gain]))


def _s5_weights(lam_re, lam_im, log_step, b_re, b_im, c_re, c_im, nb):
    G, P, J = b_re.shape[1:]
    gq = LANES // J
    nq = G // gq
    lr = jnp.minimum(lam_re, -1e-4)
    li = lam_im
    step = jnp.exp(log_step)[..., None]
    mag = jnp.exp(lr * step)
    a_re, a_im = mag * jnp.cos(li * step), mag * jnp.sin(li * step)
    den = lr * lr + li * li
    f_re = ((a_re - 1.0) * lr + a_im * li) / den
    f_im = (a_im * lr - (a_re - 1.0) * li) / den
    bb_re = f_re[..., None] * b_re - f_im[..., None] * b_im
    bb_im = f_re[..., None] * b_im + f_im[..., None] * b_re
    eye = jnp.eye(gq, dtype=F32)

    def expand(bb):
        t = bb.reshape(2, nq, gq, P, J)
        return jnp.einsum("dqgpj,gh->dqgjhp", t, eye).reshape(2, nq, gq * J, gq * P)

    def contract(cc):
        t = cc.reshape(2, nq, gq, J, P)
        return jnp.einsum("dqgjp,gh->dqgphj", t, eye).reshape(2, nq, gq * P, gq * J)

    bw = jnp.concatenate([expand(bb_re), expand(bb_im)], axis=3).astype(BF16)
    cw = jnp.concatenate([contract(c_re), -contract(c_im)], axis=2).astype(BF16)
    a_b = jnp.stack([a_re.reshape(2, G * P), a_im.reshape(2, G * P)], axis=1)
    a_b = jnp.broadcast_to(a_b[:, :, None, :], (2, 2, nb, G * P))
    return bw, cw, a_b


def _lru_gate_weights(wa, wx, ba, bx):
    nblk, bs = wa.shape[1], wa.shape[2]
    eye = jnp.eye(nblk, dtype=F32)
    dense = lambda w: jnp.einsum("dnkm,nl->dnklm", w, eye).reshape(2, nblk * bs, nblk * bs)
    wg = (0.5 * jnp.concatenate([dense(wa), dense(wx)], axis=2)).astype(BF16)
    bg = 0.5 * jnp.concatenate([ba, bx], axis=1)[:, None, :]
    return wg, bg


def _pick(n, pref):
    return pref if n % pref == 0 else n


def kernel(x, mem, mix_norm, w_in, gdn_conv, gdn_a_log, gdn_dt_bias, gdn_out_norm, s5_lambda_re, s5_lambda_im, s5_log_step, s5_b_re, s5_b_im, s5_c_re, s5_c_im, s5_d, s5_glu_w, s5_glu_b, lru_conv_w, lru_conv_b, lru_gate_a_w, lru_gate_a_b, lru_gate_x_w, lru_gate_x_b, lru_lambda, sc_conv, w_branch, w_mix_out, xa_norm, xa_mem_norm, xa_w_q, xa_w_kv, xa_w_o, ffn_norm, ffn_w_up, ffn_conv_w, ffn_conv_b, ffn_w_down, final_norm):
    Bn, S, D = x.shape
    depth = w_in.shape[0]
    W = D // 2
    H = GDN_HEADS
    T = Bn * S
    F = ffn_w_down.shape[1]
    assert Bn % BF16_ROWS == 0 and S % 256 == 0 and W == H * LANES

    blk = {"z": 3, "s5_u": 4, "lru_x": 5, "lru_g": 6, "sc_b": 7, "sc_c": 8, "sc_x": 9, "gates": 10}
    n_ab = 4 * H
    c_ab = 4 * W
    tm_proj = _pick(S, 1024)
    ts_tok = _pick(S, 512)
    rows_tm = 32 * Bn

    row2 = lambda v: v.reshape(1, -1)
    for l in range(depth):
        c_gate = c_ab + n_ab + 6 * W
        w_main = jnp.concatenate([w_in[l][:, :c_ab], w_in[l][:, c_ab + n_ab:c_gate], 0.5 * w_in[l][:, c_gate:]],
                                 axis=1).astype(BF16)
        w_ab = jnp.pad(w_in[l][:, c_ab:c_ab + n_ab], ((0, 0), (0, LANES - n_ab))).astype(BF16)
        P, ab = _proj(x.reshape(T, D), row2(mix_norm[l]), w_main, w_ab, tm_proj, _pick(w_main.shape[1], 2304))
        P3 = P.reshape(Bn, S, -1)

        qkv = _qkvprep(P3, 0.5 * gdn_conv[l], ts_tok)
        prm = jnp.zeros((8, LANES), F32)
        prm = prm.at[0, 2 * H:4 * H].set(gdn_a_log[l].reshape(-1)).at[1, 2 * H:4 * H].set(gdn_dt_bias[l].reshape(-1))
        o_f, o_b = _gdn(qkv, ab.reshape(Bn, S, LANES), prm, GDN_HEADS * GDN_CHUNK, 3 * W)

        utm = jnp.swapaxes(P3[:, :, blk["s5_u"] * W:(blk["lru_g"] + 1) * W], 0, 1).reshape(S * Bn, 3 * W)
        bw, cw, a_b = _s5_weights(s5_lambda_re[l], s5_lambda_im[l], s5_log_step[l], s5_b_re[l], s5_b_im[l],
                                  s5_c_re[l], s5_c_im[l], Bn)
        y_f, y_b = _s5(utm, bw, cw, a_b, Bn, rows_tm, 0)
        wg, bg = _lru_gate_weights(lru_gate_a_w[l], lru_gate_x_w[l], lru_gate_a_b[l], lru_gate_x_b[l])
        h_f, h_b = _lru(utm, lru_conv_w[l], row2(lru_conv_b[l]), wg, bg, lru_lambda[l][:, None, :], Bn, rows_tm, 1)
        sl3 = _post(utm, y_f, y_b, h_f, h_b, row2(s5_d[l]), s5_glu_w[l].astype(BF16), row2(s5_glu_b[l]),
                    Bn, rows_tm, 0, 2)

        x = _merge(x, o_f, o_b, P3, sl3, row2(gdn_out_norm[l]), sc_conv[l], (0.5 * w_branch[l]).astype(BF16),
                   w_mix_out[l].astype(BF16), ts_tok, blk)

        kv = _kv(mem, row2(xa_mem_norm[l]), xa_w_kv[l].astype(BF16))
        x = _xa(x, kv, row2(xa_norm[l]), xa_w_q[l].astype(BF16), xa_w_o[l].astype(BF16), ts_tok)

        x = _ffn(x, row2(ffn_norm[l]), ffn_w_up[l].astype(BF16), ffn_conv_w[l], row2(ffn_conv_b[l]),
                 ffn_w_down[l].astype(BF16), ts_tok, 256, row2(final_norm) if l == depth - 1 else None)
    return x
```

```python
import functools
import math

import jax
import jax.numpy as jnp
from jax import lax
from jax.experimental import pallas as pl
from jax.experimental.pallas import tpu as pltpu

F32 = jnp.float32
BF16 = jnp.bfloat16
EPS = 1e-6

LANES = 128
BF16_ROWS = 16
VMEM_LIMIT = 56 * 1024 * 1024

GDN_HEADS = 4
GDN_CHUNK = 64
GDN_CONV = 4
GDN_INV_PASSES = 1
LRU_CONV = 4
S5_GROUP = 16
S5_STATE = 64
LRU_C = 8.0
XA_HEADS = 4

HI = lax.Precision.HIGHEST


def _cparams(sem):
    return pltpu.CompilerParams(dimension_semantics=sem, vmem_limit_bytes=VMEM_LIMIT)


def _rms(x, g):
    return x * lax.rsqrt(jnp.mean(x * x, axis=-1, keepdims=True) + EPS) * g


def _dot(a, b):
    return jnp.dot(a, b, preferred_element_type=F32)


def _dot_nt(a, b):
    return lax.dot_general(a, b, (((1,), (1,)), ((), ())), preferred_element_type=F32)


def _sigmoid(x):
    return 0.5 * jnp.tanh(0.5 * x) + 0.5


def _softplus(x):
    return jnp.maximum(x, 0.0) + jnp.log(1.0 + jnp.exp(-jnp.abs(x)))


def _const_spec(shape):
    nd = len(shape)
    return pl.BlockSpec(shape, lambda *_: (0,) * nd)


def _proj_kernel(x_ref, g_ref, w_ref, wab_ref, p_ref, ab_ref, h_ref):
    @pl.when(pl.program_id(1) == 0)
    def _():
        hb = _rms(x_ref[...], g_ref[...]).astype(BF16)
        h_ref[...] = hb
        ab_ref[...] = _dot(hb, wab_ref[...])

    p_ref[...] = _dot(h_ref[...], w_ref[...]).astype(BF16)


def _proj(x2, gain, w_main, w_ab, tm, tn):
    T, D = x2.shape
    N = w_main.shape[1]
    return pl.pallas_call(
        _proj_kernel,
        grid=(T // tm, N // tn),
        in_specs=[
            pl.BlockSpec((tm, D), lambda i, j: (i, 0)),
            pl.BlockSpec((1, D), lambda i, j: (0, 0)),
            pl.BlockSpec((D, tn), lambda i, j: (0, j)),
            pl.BlockSpec((D, LANES), lambda i, j: (0, 0)),
        ],
        out_specs=[
            pl.BlockSpec((tm, tn), lambda i, j: (i, j)),
            pl.BlockSpec((tm, LANES), lambda i, j: (i, 0)),
        ],
        out_shape=[jax.ShapeDtypeStruct((T, N), BF16), jax.ShapeDtypeStruct((T, LANES), F32)],
        scratch_shapes=[pltpu.VMEM((tm, D), BF16)],
        compiler_params=_cparams(("parallel", "arbitrary")),
        name="proj",
    )(x2, gain, w_main, w_ab)


def _qkvprep_kernel(xm_ref, xp_ref, xn_ref, cw_ref, o_ref, *, n_norm):
    i = pl.program_id(1)
    last = pl.num_programs(1) - 1
    ts = xm_ref.shape[1]
    hal = xp_ref.shape[1]
    keep_p = (i > 0).astype(F32)
    keep_n = (i < last).astype(F32)
    ext = ts + 2 * hal
    mid = slice(hal, hal + ts)
    for c in range(xm_ref.shape[2] // LANES):
        sl = slice(c * LANES, (c + 1) * LANES)
        x = xm_ref[0, :, sl].astype(F32)
        xe = jnp.concatenate([xp_ref[0, :, sl].astype(F32) * keep_p, x, xn_ref[0, :, sl].astype(F32) * keep_n], axis=0)
        w = cw_ref[:, sl]
        hy = (w[0:1] * pltpu.roll(xe, 2, 0)[mid] + w[1:2] * pltpu.roll(xe, 1, 0)[mid] + w[2:3] * x
              + w[3:4] * pltpu.roll(xe, ext - 1, 0)[mid])
        y = hy + hy * jnp.tanh(hy)
        if c < n_norm:
            y = y * lax.rsqrt(jnp.sum(y * y, axis=-1, keepdims=True) + EPS)
        o_ref[0, :, sl] = y.astype(o_ref.dtype)


def _qkvprep(P3, conv_w, ts):
    Bn, S, _ = P3.shape
    C = conv_w.shape[1]
    hal = BF16_ROWS
    nb = ts // hal
    return pl.pallas_call(
        functools.partial(_qkvprep_kernel, n_norm=2 * GDN_HEADS),
        grid=(Bn, S // ts),
        in_specs=[
            pl.BlockSpec((1, ts, C), lambda b, i: (b, i, 0)),
            pl.BlockSpec((1, hal, C), lambda b, i: (b, jnp.maximum(i * nb - 1, 0), 0)),
            pl.BlockSpec((1, hal, C), lambda b, i: (b, jnp.minimum((i + 1) * nb, S // hal - 1), 0)),
            _const_spec(conv_w.shape),
        ],
        out_specs=pl.BlockSpec((1, ts, C), lambda b, i: (b, i, 0)),
        out_shape=jax.ShapeDtypeStruct((Bn, S, C), BF16),
        compiler_params=_cparams(("parallel", "parallel")),
        name="gdn_prep",
    )(P3, P3, P3, conv_w)


def _mm_split(lhs, w, n_pass):
    lh, wh = lhs.astype(BF16), w.astype(BF16)
    if n_pass == 1:
        return _dot(lh, wh)
    ll = (lhs - lh.astype(F32)).astype(BF16)
    wl = (w - wh.astype(F32)).astype(BF16)
    return _dot(lh, wh) + (_dot(lh, wl) + _dot(ll, wh))


def _gdn_prepare(d, qc, gam, beta, grow, msk):
    C = GDN_CHUNK
    H = GDN_HEADS
    Dh = LANES
    blk, incl, strict, eye = msk
    stack = lambda off: jnp.concatenate([qc[:, off + h * Dh: off + (h + 1) * Dh] for h in range(H)], axis=0)
    Qs = stack(0).astype(F32) * (Dh ** -0.5)
    Kb16 = stack(H * Dh)
    Ks = Kb16.astype(F32)
    Vs = stack(2 * H * Dh).astype(F32)
    gcol = jnp.concatenate(gam, axis=0)
    bcol = jnp.concatenate(beta, axis=0)
    edge = C - 1 if d == 0 else 0
    glast = [g[edge:edge + 1] for g in gam]
    glast_col = jnp.concatenate([jnp.broadcast_to(g, (C, 1)) for g in glast], axis=0)

    diff = _side_by_side(gam, blk) - grow
    decay = jnp.where(incl, jnp.exp(jnp.where(incl, diff, 0.0)), 0.0)
    kb = Ks * bcol
    g2 = _dot_nt(jnp.concatenate([kb, Qs], axis=0).astype(BF16), Kb16)
    kk = _side_by_side([g2[h * C:(h + 1) * C] for h in range(H)], blk)
    qk = _side_by_side([g2[(H + h) * C:(H + h + 1) * C] for h in range(H)], blk) * decay
    X = jnp.where(strict, -(kk * decay), 0.0)
    egam = jnp.exp(gcol)
    return dict(X=X, P=eye + X, qk=qk, rhs=jnp.concatenate([Vs * bcol, kb * egam], axis=1).astype(BF16),
                qdec=Qs * egam, kdec=Ks * jnp.exp(glast_col - gcol), eg=[jnp.exp(g) for g in glast])


def _side_by_side(cols, blk):
    out = jnp.where(blk[0], cols[0], 0.0)
    for h in range(1, len(blk)):
        out = jnp.where(blk[h], cols[h], out)
    return out


def _bd(m, blk):
    return jnp.concatenate([jnp.where(b, m, 0.0) for b in blk], axis=0)


def _gdn_inverse_step(c, k, blk):
    C = GDN_CHUNK
    w = _bd(c["X"], blk)
    if k == 0:
        c["X"] = _mm_split(c["X"], w, GDN_INV_PASSES)
    elif k < 5:
        r = _mm_split(jnp.concatenate([c["P"], c["X"]], axis=0), w, GDN_INV_PASSES)
        c["P"] = c["P"] + r[:C]
        c["X"] = r[C:]
    else:
        c["P"] = c["P"] + _mm_split(c["P"], w, GDN_INV_PASSES)


def _gdn_wy(c, blk):
    C = GDN_CHUNK
    H = GDN_HEADS
    wy = _dot(_bd(c["P"], blk).astype(BF16), c["rhs"])
    c["U"], Wm = wy[:, :LANES], wy[:, LANES:]
    qdec = c["qdec"]
    c["lhs"] = [jnp.concatenate([Wm[h * C:(h + 1) * C], qdec[h * C:(h + 1) * C]], axis=0).astype(BF16)
                for h in range(H)]
    c["kT"] = c["kdec"].T.astype(BF16)
    c["egrow"] = jnp.concatenate([jnp.broadcast_to(e, (1, LANES)) for e in c["eg"]], axis=1)
    c["qkbd"] = _bd(c["qk"], blk).astype(BF16)


def _gdn_kernel(qf_ref, af_ref, qb_ref, abk_ref, prm_ref, of_ref, ob_ref, sf_ref, sb_ref):
    C = GDN_CHUNK
    H = GDN_HEADS
    CT = qf_ref.shape[1]
    nch = CT // C

    @pl.when(pl.program_id(1) == 0)
    def _():
        sf_ref[...] = jnp.zeros_like(sf_ref)
        sb_ref[...] = jnp.zeros_like(sb_ref)

    ri = lax.broadcasted_iota(jnp.int32, (C, H * C), 0)
    lane = lax.broadcasted_iota(jnp.int32, (C, H * C), 1)
    cj = lane % C
    blk = [(lane // C) == h for h in range(H)]
    eye = (cj == ri).astype(F32)
    rt = lax.broadcasted_iota(jnp.int32, (CT, CT), 0)
    ct = lax.broadcasted_iota(jnp.int32, (CT, CT), 1)
    same_t = (rt // C) == (ct // C)
    neg_a = -jnp.exp(prm_ref[0:1, :])
    dt_b = prm_ref[1:2, :]

    dirs = ((0, qf_ref, af_ref, of_ref, sf_ref), (1, qb_ref, abk_ref, ob_ref, sb_ref))

    chunks = {}
    for d, q_ref, a_ref, o_ref, s_ref in dirs:
        ab = a_ref[0]
        bt = _sigmoid(ab)
        la = neg_a * _softplus(ab + dt_b)
        cum = (same_t & ((ct <= rt) if d == 0 else (ct >= rt))).astype(F32)
        gam = jnp.dot(cum, la, precision=HI, preferred_element_type=F32)
        slab = gam.T[2 * H:4 * H, :]
        msk = (blk, (cj <= ri) if d == 0 else (cj >= ri), (cj < ri) if d == 0 else (cj > ri), eye)
        for ci in range(nch):
            rows = slice(ci * C, (ci + 1) * C)
            gl = [gam[rows, 2 * H + d * H + h: 2 * H + d * H + h + 1] for h in range(H)]
            bl = [bt[rows, d * H + h: d * H + h + 1] for h in range(H)]
            pieces = []
            for h in range(H):
                shift = ((h - ci) * C) % CT
                moved = slab if shift == 0 else pltpu.roll(slab, shift, 1)
                pieces.append(moved[d * H + h:d * H + h + 1, :])
            grow = _side_by_side(pieces, [b[0:1, :] for b in blk])
            chunks[d, ci] = _gdn_prepare(d, q_ref[0, rows, :], gl, bl, grow, msk)

    order = [(d, step if d == 0 else nch - 1 - step) for step in range(nch) for d in (0, 1)]
    for k in range(6):
        for key in order:
            _gdn_inverse_step(chunks[key], k, blk)
    for key in order:
        _gdn_wy(chunks[key], blk)

    state = {d: jnp.concatenate([s_ref[h] for h in range(H)], axis=1) for d, _, _, _, s_ref in dirs}
    zero = jnp.zeros((C, LANES), BF16)
    for step in range(nch):
        cur = {d: chunks[d, step if d == 0 else nch - 1 - step] for d in (0, 1)}
        r = {d: [_dot(cur[d]["lhs"][h], state[d][:, h * LANES:(h + 1) * LANES].astype(BF16)) for h in range(H)]
             for d in (0, 1)}
        vbd = {}
        for d, _, _, o_ref, _ in dirs:
            c = cur[d]
            vnew = [(c["U"][h * C:(h + 1) * C] - r[d][h][:C]).astype(BF16) for h in range(H)]
            out = jnp.concatenate([r[d][h][C:] for h in range(H)], axis=0) + _dot(c["qkbd"], jnp.concatenate(vnew, axis=0))
            ci = step if d == 0 else nch - 1 - step
            for h in range(H):
                o_ref[0, ci * C:(ci + 1) * C, h * LANES:(h + 1) * LANES] = out[h * C:(h + 1) * C].astype(o_ref.dtype)
            vbd[d] = jnp.concatenate([jnp.concatenate([vnew[h] if g == h else zero for g in range(H)], axis=1)
                                      for h in range(H)], axis=0)
        for d in (0, 1):
            state[d] = state[d] * cur[d]["egrow"] + _dot(cur[d]["kT"], vbd[d])
    for d, _, _, _, s_ref in dirs:
        for h in range(H):
            s_ref[h] = state[d][:, h * LANES:(h + 1) * LANES]


def _gdn(qkv, ab3, prm, ct, C3):
    Bn, S, _ = qkv.shape
    W = C3 // 3
    nt = S // ct
    dh = W // GDN_HEADS
    assert ct == GDN_HEADS * GDN_CHUNK and dh == LANES
    fwd = lambda b, n: (b, n, 0)
    bwd = lambda b, n: (b, nt - 1 - n, 0)
    return pl.pallas_call(
        _gdn_kernel,
        grid=(Bn, nt),
        in_specs=[
            pl.BlockSpec((1, ct, C3), fwd),
            pl.BlockSpec((1, ct, LANES), fwd),
            pl.BlockSpec((1, ct, C3), bwd),
            pl.BlockSpec((1, ct, LANES), bwd),
            _const_spec(prm.shape),
        ],
        out_specs=[pl.BlockSpec((1, ct, W), fwd), pl.BlockSpec((1, ct, W), bwd)],
        out_shape=[jax.ShapeDtypeStruct((Bn, S, W), BF16)] * 2,
        scratch_shapes=[pltpu.VMEM((GDN_HEADS, dh, dh), F32)] * 2,
        compiler_params=_cparams(("parallel", "arbitrary")),
        name="gdn",
    )(qkv, ab3, qkv, ab3, prm)


def _s5_kernel(uf_ref, ub_ref, bw_ref, cw_ref, a_ref, yf_ref, yb_ref, bu_ref, hb_ref, carry_ref, *, nb):
    R = uf_ref.shape[0]
    tt = R // nb
    nq = bw_ref.shape[1]
    ns = bw_ref.shape[3]
    half = ns // 2
    n_tile = 2 * LANES
    u_refs, y_refs = (uf_ref, ub_ref), (yf_ref, yb_ref)
    units = [(d, q) for d in (0, 1) for q in range(nq)]

    @pl.when(pl.program_id(0) == 0)
    def _():
        carry_ref[...] = jnp.zeros_like(carry_ref)

    def expand(k, n):
        d, q = units[k]
        cols = slice(n * n_tile, (n + 1) * n_tile)
        bu_ref[k % 2, :, cols] = _dot(u_refs[d][:, q * LANES:(q + 1) * LANES], bw_ref[d, q, :, cols])

    def contract(k, part, parts=2):
        d, q = units[k]
        rows = slice(part * R // parts, (part + 1) * R // parts)
        y_refs[d][rows, q * LANES:(q + 1) * LANES] = _dot(hb_ref[k % 2, rows, :], cw_ref[d, q]).astype(y_refs[d].dtype)

    for n in range(ns // n_tile):
        expand(0, n)
    for k, (d, q) in enumerate(units):
        mxu = []
        if k + 1 < len(units):
            mxu += [functools.partial(expand, k + 1, n) for n in range(ns // n_tile)]
        if k > 0:
            mxu += [functools.partial(contract, k - 1, p) for p in range(2)]
        every = -(-tt // len(mxu))
        ar = a_ref[d, 0, :, q * half:(q + 1) * half]
        ai = a_ref[d, 1, :, q * half:(q + 1) * half]
        cre = slice(q * ns, q * ns + half)
        cim = slice(q * ns + half, (q + 1) * ns)
        hr, hi = carry_ref[d, :, cre], carry_ref[d, :, cim]
        for s in range(tt):
            if mxu and s % every == 0:
                mxu.pop(0)()
            t = s if d == 0 else tt - 1 - s
            rows = slice(t * nb, (t + 1) * nb)
            hr, hi = (ar * hr - ai * hi + bu_ref[k % 2, rows, :half],
                      ar * hi + ai * hr + bu_ref[k % 2, rows, half:])
            hb_ref[k % 2, rows, :half] = hr.astype(BF16)
            hb_ref[k % 2, rows, half:] = hi.astype(BF16)
        for piece in mxu:
            piece()
        carry_ref[d, :, cre] = hr
        carry_ref[d, :, cim] = hi
    for p in range(2):
        contract(len(units) - 1, p)


def _s5(utm, bw, cw, a_b, nb, rows, col_blk):
    TB = utm.shape[0]
    nq = bw.shape[1]
    W = nq * LANES
    nstate = bw.shape[3] * nq
    nt = TB // rows
    fwd = lambda n: (n, col_blk)
    bwd = lambda n: (nt - 1 - n, col_blk)
    return pl.pallas_call(
        functools.partial(_s5_kernel, nb=nb),
        grid=(nt,),
        in_specs=[
            pl.BlockSpec((rows, W), fwd),
            pl.BlockSpec((rows, W), bwd),
            _const_spec(bw.shape),
            _const_spec(cw.shape),
            _const_spec(a_b.shape),
        ],
        out_specs=[pl.BlockSpec((rows, W), lambda n: (n, 0)), pl.BlockSpec((rows, W), lambda n: (nt - 1 - n, 0))],
        out_shape=[jax.ShapeDtypeStruct((TB, W), BF16)] * 2,
        scratch_shapes=[pltpu.VMEM((2, rows, bw.shape[3]), F32), pltpu.VMEM((2, rows, bw.shape[3]), BF16),
                        pltpu.VMEM((2, nb, nstate), F32)],
        compiler_params=_cparams(("arbitrary",)),
        name="s5_scan",
    )(utm, utm, bw, cw, a_b)


def _lru_kernel(xf_ref, xfp_ref, xfn_ref, xb_ref, xbp_ref, xbn_ref, cw_ref, cb_ref, wg_ref, bg_ref, lam_ref,
                of_ref, ob_ref, a_s, b_s, carry_ref, *, nb):
    n = pl.program_id(0)
    nt = pl.num_programs(0)
    R = xf_ref.shape[0]
    W = xf_ref.shape[1]
    tt = R // nb

    @pl.when(n == 0)
    def _():
        carry_ref[...] = jnp.zeros_like(carry_ref)

    streams = ((0, n, xf_ref, xfp_ref, xfn_ref, of_ref), (1, nt - 1 - n, xb_ref, xbp_ref, xbn_ref, ob_ref))
    for d, pos, xm_ref, xp_ref, xn_ref, o_ref in streams:
        xp = xp_ref[...].astype(F32) * (pos > 0).astype(F32)
        xn = xn_ref[...].astype(F32) * (pos < nt - 1).astype(F32)
        xe = jnp.concatenate([xp, xm_ref[...].astype(F32), xn], axis=0)
        w = cw_ref[...]
        xc = cb_ref[...] + sum(w[k:k + 1] * xe[k * nb:k * nb + R] for k in range(LRU_CONV))
        th = jnp.tanh(_dot(xc.astype(BF16), wg_ref[d]) + bg_ref[d])
        ig = 0.5 * th[:, W:] + 0.5
        c2 = (-0.5 * LRU_C) * _softplus(-lam_ref[d])
        a = jnp.exp(c2 * th[:, :W] + c2)
        a_s[d] = a
        b_s[d] = jnp.sqrt(1.0 - a * a) * (ig * xc)

        def body(s, h, d=d, o_ref=o_ref):
            t = s if d == 0 else tt - 1 - s
            rows = pl.ds(pl.multiple_of(t * nb, nb), nb)
            h = a_s[d, rows, :] * h + b_s[d, rows, :]
            o_ref[rows, :] = h.astype(o_ref.dtype)
            return h

        carry_ref[d] = lax.fori_loop(0, tt, body, carry_ref[d], unroll=4)


def _lru(utm, conv_w, conv_b, wg, bg, lam, nb, rows, col_blk):
    TB = utm.shape[0]
    W = conv_w.shape[1]
    nt = TB // rows
    hp = 2 * nb
    hn = nb
    nbt = TB // nb

    def specs(pos):
        return [
            pl.BlockSpec((rows, W), lambda n: (pos(n), col_blk)),
            pl.BlockSpec((hp, W), lambda n: (jnp.maximum(pos(n) * (rows // hp) - 1, 0), col_blk)),
            pl.BlockSpec((hn, W), lambda n: (jnp.minimum((pos(n) + 1) * (rows // hn), nbt - 1), col_blk)),
        ]

    fwd = lambda n: n
    bwd = lambda n: nt - 1 - n
    return pl.pallas_call(
        functools.partial(_lru_kernel, nb=nb),
        grid=(nt,),
        in_specs=specs(fwd) + specs(bwd) + [_const_spec(conv_w.shape), _const_spec(conv_b.shape),
                                            _const_spec(wg.shape), _const_spec(bg.shape), _const_spec(lam.shape)],
        out_specs=[pl.BlockSpec((rows, W), lambda n: (n, 0)), pl.BlockSpec((rows, W), lambda n: (nt - 1 - n, 0))],
        out_shape=[jax.ShapeDtypeStruct((TB, W), BF16)] * 2,
        scratch_shapes=[pltpu.VMEM((2, rows, W), F32), pltpu.VMEM((2, rows, W), F32), pltpu.VMEM((2, nb, W), F32)],
        compiler_params=_cparams(("arbitrary",)),
        name="lru_scan",
    )(utm, utm, utm, utm, utm, utm, conv_w, conv_b, wg, bg, lam)


def _post_kernel(u_ref, g_ref, yf_ref, yb_ref, hf_ref, hb_ref, d_ref, gw_ref, gb_ref, perm_ref, o_ref):
    nb, tt = o_ref.shape[0], o_ref.shape[1]
    y = u_ref[...].astype(F32) * d_ref[...] + yf_ref[...].astype(F32) + yb_ref[...].astype(F32)
    zg = jax.nn.gelu(y)
    s5 = zg * _sigmoid(_dot(zg.astype(BF16), gw_ref[...]) + gb_ref[...])
    h = hf_ref[...].astype(F32) + hb_ref[...].astype(F32)
    lru = h * jax.nn.gelu(g_ref[...].astype(F32))
    res = jnp.concatenate([s5, lru], axis=1).astype(BF16)
    tok = _dot(perm_ref[...], res).astype(o_ref.dtype)
    for b in range(nb):
        o_ref[b] = tok[b * tt:(b + 1) * tt]


def _post(utm, yf, yb, hf, hb, s5_d, glu_w, glu_b, nb, rows, u_blk, g_blk):
    TB, W = yf.shape
    tt = rows // nb
    r = jnp.arange(rows)
    perm = (r[None, :] == ((r % tt) * nb + r // tt)[:, None]).astype(BF16)
    row = lambda n: (n, 0)
    return pl.pallas_call(
        _post_kernel,
        grid=(TB // rows,),
        in_specs=[
            pl.BlockSpec((rows, W), lambda n: (n, u_blk)),
            pl.BlockSpec((rows, W), lambda n: (n, g_blk)),
            pl.BlockSpec((rows, W), row), pl.BlockSpec((rows, W), row),
            pl.BlockSpec((rows, W), row), pl.BlockSpec((rows, W), row),
            _const_spec(s5_d.shape), _const_spec(glu_w.shape), _const_spec(glu_b.shape), _const_spec(perm.shape),
        ],
        out_specs=pl.BlockSpec((nb, tt, 2 * W), lambda n: (0, n, 0)),
        out_shape=jax.ShapeDtypeStruct((nb, TB // nb, 2 * W), BF16),
        compiler_params=_cparams(("parallel",)),
        name="s5_lru_post",
    )(utm, utm, yf, yb, hf, hb, s5_d, glu_w, glu_b, perm)


def _merge_kernel(x_ref, of_ref, ob_ref, z_ref, sl_ref, scb_ref, scc_ref, scx_ref, ccp_ref, cxp_ref, ccn_ref,
                  cxn_ref, g0_ref, g1_ref, g2_ref, g3_ref, gn_ref, scw_ref, wb_ref, wo_ref, o_ref):
    i = pl.program_id(1)
    last = pl.num_programs(1) - 1
    tm = x_ref.shape[1]
    W = z_ref.shape[2]
    hal = ccp_ref.shape[1]

    cx = scc_ref[0].astype(F32) * scx_ref[0].astype(F32)
    cx_p = (ccp_ref[0, hal - 1:hal].astype(F32) * cxp_ref[0, hal - 1:hal].astype(F32)) * (i > 0).astype(F32)
    cx_n = (ccn_ref[0, 0:1].astype(F32) * cxn_ref[0, 0:1].astype(F32)) * (i < last).astype(F32)
    row = lax.broadcasted_iota(jnp.int32, (tm, 1), 0)
    m1 = jnp.where(row == 0, cx_p, pltpu.roll(cx, 1, 0))
    p1 = jnp.where(row == tm - 1, cx_n, pltpu.roll(cx, tm - 1, 0))
    w = scw_ref[...]
    conv = w[0:1] * m1 + w[1:2] * cx + w[2:3] * p1
    g_refs = (g0_ref, g1_ref, g2_ref, g3_ref)

    def gated_sum(rs):
        o = of_ref[0, rs, :].astype(F32) + ob_ref[0, rs, :].astype(F32)
        z = z_ref[0, rs, :].astype(F32)
        ys = []
        for h in range(GDN_HEADS):
            oh = o[:, h * LANES:(h + 1) * LANES]
            ys.append(oh * lax.rsqrt(jnp.mean(oh * oh, axis=-1, keepdims=True) + EPS) * gn_ref[...])
        y_gdn = jnp.concatenate(ys, axis=1) * (z * _sigmoid(z))
        y_sc = scb_ref[0, rs, :].astype(F32) * conv[rs]
        branches = (y_gdn.astype(BF16), sl_ref[0, rs, :W], sl_ref[0, rs, W:], y_sc.astype(BF16))
        merged = None
        for m, y in enumerate(branches):
            hd = _dot(y, wb_ref[m])
            t = hd + hd * jnp.tanh(g_refs[m][0, rs, :].astype(F32))
            merged = t if merged is None else merged + t
        return merged.astype(BF16)

    half = tm // 2
    rows = [slice(0, half), slice(half, tm)]
    merged = gated_sum(rows[0])
    for k in range(2):
        nxt = gated_sum(rows[k + 1]) if k == 0 else None
        o_ref[0, rows[k], :] = x_ref[0, rows[k], :] + _dot(merged, wo_ref[...])
        merged = nxt


def _merge(x3, o_f, o_b, P3, sl3, gdn_gain, sc_w, w_branch, w_out, tm, blk):
    Bn, S, D = x3.shape
    W = o_f.shape[2]
    hal = BF16_ROWS
    nb = tm // hal
    tok = lambda b, i: (b, i, 0)
    col = lambda k: (lambda b, i: (b, i, k))
    prev = lambda k: (lambda b, i: (b, jnp.maximum(i * nb - 1, 0), k))
    nxt = lambda k: (lambda b, i: (b, jnp.minimum((i + 1) * nb, S // hal - 1), k))
    gate0 = blk["gates"] * W // D
    return pl.pallas_call(
        _merge_kernel,
        grid=(Bn, S // tm),
        in_specs=[
            pl.BlockSpec((1, tm, D), tok),
            pl.BlockSpec((1, tm, W), tok), pl.BlockSpec((1, tm, W), tok),
            pl.BlockSpec((1, tm, W), col(blk["z"])),
            pl.BlockSpec((1, tm, 2 * W), tok),
            pl.BlockSpec((1, tm, W), col(blk["sc_b"])),
            pl.BlockSpec((1, tm, W), col(blk["sc_c"])),
            pl.BlockSpec((1, tm, W), col(blk["sc_x"])),
            pl.BlockSpec((1, hal, W), prev(blk["sc_c"])), pl.BlockSpec((1, hal, W), prev(blk["sc_x"])),
            pl.BlockSpec((1, hal, W), nxt(blk["sc_c"])), pl.BlockSpec((1, hal, W), nxt(blk["sc_x"])),
            pl.BlockSpec((1, tm, D), col(gate0)), pl.BlockSpec((1, tm, D), col(gate0 + 1)),
            pl.BlockSpec((1, tm, D), col(gate0 + 2)), pl.BlockSpec((1, tm, D), col(gate0 + 3)),
            _const_spec(gdn_gain.shape), _const_spec(sc_w.shape), _const_spec(w_branch.shape),
            _const_spec(w_out.shape),
        ],
        out_specs=pl.BlockSpec((1, tm, D), tok),
        out_shape=jax.ShapeDtypeStruct((Bn, S, D), F32),
        compiler_params=_cparams(("parallel", "parallel")),
        name="merge",
    )(x3, o_f, o_b, P3, sl3, P3, P3, P3, P3, P3, P3, P3, P3, P3, P3, P3, gdn_gain, sc_w, w_branch, w_out)


def _kv_kernel(m_ref, g_ref, w_ref, o_ref):
    o_ref[0] = _dot(_rms(m_ref[0], g_ref[...]).astype(BF16), w_ref[...]).astype(o_ref.dtype)


def _kv(mem, gain, w_kv):
    Bn, M, D = mem.shape
    return pl.pallas_call(
        _kv_kernel,
        grid=(Bn,),
        in_specs=[pl.BlockSpec((1, M, D), lambda b: (b, 0, 0)), _const_spec(gain.shape), _const_spec(w_kv.shape)],
        out_specs=pl.BlockSpec((1, M, 2 * D), lambda b: (b, 0, 0)),
        out_shape=jax.ShapeDtypeStruct((Bn, M, 2 * D), BF16),
        compiler_params=_cparams(("parallel",)),
        name="xa_kv",
    )(mem, gain, w_kv)


def _xa_kernel(x_ref, kv_ref, g_ref, wq_ref, wo_ref, o_ref):
    D = x_ref.shape[2]
    dh = D // XA_HEADS
    x = x_ref[0]
    q = _dot(_rms(x, g_ref[...]).astype(BF16), wq_ref[...]).astype(BF16)
    def scores(h):
        return _dot_nt(q[:, h * dh:(h + 1) * dh], kv_ref[0, :, h * dh:(h + 1) * dh]) * (dh ** -0.5)

    def softmax(s):
        e = jnp.exp(s - jnp.max(s, axis=-1, keepdims=True))
        return (e / jnp.sum(e, axis=-1, keepdims=True)).astype(BF16)

    def values(h, p):
        return _dot(p, kv_ref[0, :, D + h * dh:D + (h + 1) * dh])

    H = XA_HEADS
    s, p, outs = {}, {}, {}
    s[0] = scores(0)
    for h in range(H + 2):
        if h + 1 < H:
            s[h + 1] = scores(h + 1)
        if h < H:
            p[h] = softmax(s[h])
        if 1 <= h <= H:
            outs[h - 1] = values(h - 1, p[h - 1])
    o = jnp.concatenate([outs[h] for h in range(H)], axis=1).astype(BF16)
    o_ref[0] = x + _dot(o, wo_ref[...])


def _xa(x3, kv, gain, wq, wo, tm):
    Bn, S, D = x3.shape
    M = kv.shape[1]
    tok = lambda b, i: (b, i, 0)
    return pl.pallas_call(
        _xa_kernel,
        grid=(Bn, S // tm),
        in_specs=[pl.BlockSpec((1, tm, D), tok), pl.BlockSpec((1, M, 2 * D), lambda b, i: (b, 0, 0)),
                  _const_spec(gain.shape), _const_spec(wq.shape), _const_spec(wo.shape)],
        out_specs=pl.BlockSpec((1, tm, D), tok),
        out_shape=jax.ShapeDtypeStruct((Bn, S, D), F32),
        compiler_params=_cparams(("parallel", "parallel")),
        name="xattn",
    )(x3, kv, gain, wq, wo)


def _ffn_kernel(x_ref, xp_ref, xn_ref, g_ref, wu_ref, cw_ref, cb_ref, wd_ref, *rest, tf):
    fg_ref = rest[0] if len(rest) == 3 else None
    o_ref, act_ref = rest[-2:]
    i = pl.program_id(1)
    last = pl.num_programs(1) - 1
    tm = x_ref.shape[1]
    hal = xp_ref.shape[1]
    F = wd_ref.shape[0]
    x = x_ref[0]
    g = g_ref[...]
    hp = _rms(xp_ref[0], g) * (i > 0).astype(F32)
    hn = _rms(xn_ref[0], g) * (i < last).astype(F32)
    he = jnp.concatenate([hp, _rms(x, g), hn], axis=0).astype(BF16)
    ext = tm + 2 * hal

    def conv(u, cols):
        w = cw_ref[:, cols]
        m1 = pltpu.roll(u, 1, 0)[hal:hal + tm]
        p1 = pltpu.roll(u, ext - 1, 0)[hal:hal + tm]
        return w[0:1] * m1 + w[1:2] * u[hal:hal + tm] + w[2:3] * p1 + cb_ref[:, cols]

    for f in range(F // tf):
        gc = slice(f * tf, (f + 1) * tf)
        uc = slice(F + f * tf, F + (f + 1) * tf)
        gate = conv(_dot(he, wu_ref[:, gc]), gc)
        up = conv(_dot(he, wu_ref[:, uc]), uc)
        act_ref[:, gc] = (gate * _sigmoid(gate) * up).astype(BF16)
    out = x + _dot(act_ref[...], wd_ref[...])
    o_ref[0] = out if fg_ref is None else _rms(out, fg_ref[...])


def _ffn(x3, gain, w_up, conv_w, conv_b, w_down, tm, tf, final_gain=None):
    Bn, S, D = x3.shape
    hal = 8
    nb = tm // hal
    tok = lambda b, i: (b, i, 0)
    resident = lambda shape: pl.BlockSpec(shape, lambda b, i: (0, 0), pipeline_mode=pl.Buffered(1))
    return pl.pallas_call(
        functools.partial(_ffn_kernel, tf=tf),
        grid=(Bn, S // tm),
        in_specs=[
            pl.BlockSpec((1, tm, D), tok),
            pl.BlockSpec((1, hal, D), lambda b, i: (b, jnp.maximum(i * nb - 1, 0), 0)),
            pl.BlockSpec((1, hal, D), lambda b, i: (b, jnp.minimum((i + 1) * nb, S // hal - 1), 0)),
            _const_spec(gain.shape), resident(w_up.shape), _const_spec(conv_w.shape), _const_spec(conv_b.shape),
            resident(w_down.shape),
        ] + ([] if final_gain is None else [_const_spec(final_gain.shape)]),
        out_specs=pl.BlockSpec((1, tm, D), tok),
        out_shape=jax.ShapeDtypeStruct((Bn, S, D), F32),
        scratch_shapes=[pltpu.VMEM((tm, w_down.shape[0]), BF16)],
        compiler_params=_cparams(("parallel", "parallel")),
        name="ffn",
    )(x3, x3, x3, gain, w_up, conv_w, conv_b, w_down, *([] if final_gain is None else [final_gain]))


def _s5_weights(lam_re, lam_im, log_step, b_re, b_im, c_re, c_im, nb):
    G, P, J = b_re.shape[1:]
    gq = LANES // J
    nq = G // gq
    lr = jnp.minimum(lam_re, -1e-4)
    li = lam_im
    step = jnp.exp(log_step)[..., None]
    mag = jnp.exp(lr * step)
    a_re, a_im = mag * jnp.cos(li * step), mag * jnp.sin(li * step)
    den = lr * lr + li * li
    f_re = ((a_re - 1.0) * lr + a_im * li) / den
    f_im = (a_im * lr - (a_re - 1.0) * li) / den
    bb_re = f_re[..., None] * b_re - f_im[..., None] * b_im
    bb_im = f_re[..., None] * b_im + f_im[..., None] * b_re
    eye = jnp.eye(gq, dtype=F32)

    def expand(bb):
        t = bb.reshape(2, nq, gq, P, J)
        return jnp.einsum("dqgpj,gh->dqgjhp", t, eye).reshape(2, nq, gq * J, gq * P)

    def contract(cc):
        t = cc.reshape(2, nq, gq, J, P)
        return jnp.einsum("dqgjp,gh->dqgphj", t, eye).reshape(2, nq, gq * P, gq * J)

    bw = jnp.concatenate([expand(bb_re), expand(bb_im)], axis=3).astype(BF16)
    cw = jnp.concatenate([contract(c_re), -contract(c_im)], axis=2).astype(BF16)
    a_b = jnp.stack([a_re.reshape(2, G * P), a_im.reshape(2, G * P)], axis=1)
    a_b = jnp.broadcast_to(a_b[:, :, None, :], (2, 2, nb, G * P))
    return bw, cw, a_b


def _lru_gate_weights(wa, wx, ba, bx):
    nblk, bs = wa.shape[1], wa.shape[2]
    eye = jnp.eye(nblk, dtype=F32)
    dense = lambda w: jnp.einsum("dnkm,nl->dnklm", w, eye).reshape(2, nblk * bs, nblk * bs)
    wg = (0.5 * jnp.concatenate([dense(wa), dense(wx)], axis=2)).astype(BF16)
    bg = 0.5 * jnp.concatenate([ba, bx], axis=1)[:, None, :]
    return wg, bg


def _pick(n, pref):
    return pref if n % pref == 0 else n


def kernel(x, mem, mix_norm, w_in, gdn_conv, gdn_a_log, gdn_dt_bias, gdn_out_norm, s5_lambda_re, s5_lambda_im, s5_log_step, s5_b_re, s5_b_im, s5_c_re, s5_c_im, s5_d, s5_glu_w, s5_glu_b, lru_conv_w, lru_conv_b, lru_gate_a_w, lru_gate_a_b, lru_gate_x_w, lru_gate_x_b, lru_lambda, sc_conv, w_branch, w_mix_out, xa_norm, xa_mem_norm, xa_w_q, xa_w_kv, xa_w_o, ffn_norm, ffn_w_up, ffn_conv_w, ffn_conv_b, ffn_w_down, final_norm):
    Bn, S, D = x.shape
    depth = w_in.shape[0]
    W = D // 2
    H = GDN_HEADS
    T = Bn * S
    F = ffn_w_down.shape[1]
    assert Bn % BF16_ROWS == 0 and S % 256 == 0 and W == H * LANES

    blk = {"z": 3, "s5_u": 4, "lru_x": 5, "lru_g": 6, "sc_b": 7, "sc_c": 8, "sc_x": 9, "gates": 10}
    n_ab = 4 * H
    c_ab = 4 * W
    tm_proj = _pick(S, 1024)
    ts_tok = _pick(S, 512)
    ts_big = _pick(S, 1024)
    rows_post = 32 * Bn
    rows_scan = (64 if S % 64 == 0 else 32) * Bn

    row2 = lambda v: v.reshape(1, -1)
    for l in range(depth):
        c_gate = c_ab + n_ab + 6 * W
        w_main = jnp.concatenate([w_in[l][:, :c_ab], w_in[l][:, c_ab + n_ab:c_gate], 0.5 * w_in[l][:, c_gate:]],
                                 axis=1).astype(BF16)
        w_ab = jnp.pad(w_in[l][:, c_ab:c_ab + n_ab], ((0, 0), (0, LANES - n_ab))).astype(BF16)
        P, ab = _proj(x.reshape(T, D), row2(mix_norm[l]), w_main, w_ab, tm_proj, _pick(w_main.shape[1], 3072))
        P3 = P.reshape(Bn, S, -1)

        qkv = _qkvprep(P3, 0.5 * gdn_conv[l], ts_tok)
        prm = jnp.zeros((8, LANES), F32)
        prm = prm.at[0, 2 * H:4 * H].set(gdn_a_log[l].reshape(-1)).at[1, 2 * H:4 * H].set(gdn_dt_bias[l].reshape(-1))
        o_f, o_b = _gdn(qkv, ab.reshape(Bn, S, LANES), prm, GDN_HEADS * GDN_CHUNK, 3 * W)

        utm = jnp.swapaxes(P3[:, :, blk["s5_u"] * W:(blk["lru_g"] + 1) * W], 0, 1).reshape(S * Bn, 3 * W)
        bw, cw, a_b = _s5_weights(s5_lambda_re[l], s5_lambda_im[l], s5_log_step[l], s5_b_re[l], s5_b_im[l],
                                  s5_c_re[l], s5_c_im[l], Bn)
        y_f, y_b = _s5(utm, bw, cw, a_b, Bn, rows_scan, 0)
        wg, bg = _lru_gate_weights(lru_gate_a_w[l], lru_gate_x_w[l], lru_gate_a_b[l], lru_gate_x_b[l])
        h_f, h_b = _lru(utm, lru_conv_w[l], row2(lru_conv_b[l]), wg, bg, lru_lambda[l][:, None, :], Bn, rows_scan, 1)
        sl3 = _post(utm, y_f, y_b, h_f, h_b, row2(s5_d[l]), s5_glu_w[l].astype(BF16), row2(s5_glu_b[l]),
                    Bn, rows_post, 0, 2)

        x = _merge(x, o_f, o_b, P3, sl3, row2(gdn_out_norm[l]), sc_conv[l], (0.5 * w_branch[l]).astype(BF16),
                   w_mix_out[l].astype(BF16), ts_tok, blk)

        kv = _kv(mem, row2(xa_mem_norm[l]), xa_w_kv[l].astype(BF16))
        x = _xa(x, kv, row2(xa_norm[l]), xa_w_q[l].astype(BF16), xa_w_o[l].astype(BF16), ts_big)

        x = _ffn(x, row2(ffn_norm[l]), ffn_w_up[l].astype(BF16), ffn_conv_w[l], row2(ffn_conv_b[l]),
                 ffn_w_down[l].astype(BF16), ts_tok, 256, row2(final_norm) if l == depth - 1 else None)
    return x
```

```python
import functools
import math

import jax
import jax.numpy as jnp
from jax import lax
from jax.experimental import pallas as pl
from jax.experimental.pallas import tpu as pltpu

F32 = jnp.float32
BF16 = jnp.bfloat16
EPS = 1e-6

LANES = 128
BF16_ROWS = 16
VMEM_LIMIT = 56 * 1024 * 1024

GDN_HEADS = 4
GDN_CHUNK = 64
GDN_CONV = 4
GDN_INV_PASSES = 1
LRU_CONV = 4
S5_GROUP = 16
S5_STATE = 64
LRU_C = 8.0
XA_HEADS = 4

HI = lax.Precision.HIGHEST


def _cparams(sem):
    return pltpu.CompilerParams(dimension_semantics=sem, vmem_limit_bytes=VMEM_LIMIT)


def _rms(x, g):
    return x * lax.rsqrt(jnp.mean(x * x, axis=-1, keepdims=True) + EPS) * g


def _dot(a, b):
    return jnp.dot(a, b, preferred_element_type=F32)


def _dot_nt(a, b):
    return lax.dot_general(a, b, (((1,), (1,)), ((), ())), preferred_element_type=F32)


def _sigmoid(x):
    return 0.5 * jnp.tanh(0.5 * x) + 0.5


def _softplus(x):
    return jnp.maximum(x, 0.0) + jnp.log(1.0 + jnp.exp(-jnp.abs(x)))


def _const_spec(shape):
    nd = len(shape)
    return pl.BlockSpec(shape, lambda *_: (0,) * nd)


def _proj_kernel(x_ref, g_ref, w_ref, wab_ref, p_ref, ab_ref, h_ref):
    @pl.when(pl.program_id(1) == 0)
    def _():
        hb = _rms(x_ref[...], g_ref[...]).astype(BF16)
        h_ref[...] = hb
        ab_ref[...] = _dot(hb, wab_ref[...])

    p_ref[...] = _dot(h_ref[...], w_ref[...]).astype(BF16)


def _proj(x2, gain, w_main, w_ab, tm, tn):
    T, D = x2.shape
    N = w_main.shape[1]
    return pl.pallas_call(
        _proj_kernel,
        grid=(T // tm, N // tn),
        in_specs=[
            pl.BlockSpec((tm, D), lambda i, j: (i, 0)),
            pl.BlockSpec((1, D), lambda i, j: (0, 0)),
            pl.BlockSpec((D, tn), lambda i, j: (0, j)),
            pl.BlockSpec((D, LANES), lambda i, j: (0, 0)),
        ],
        out_specs=[
            pl.BlockSpec((tm, tn), lambda i, j: (i, j)),
            pl.BlockSpec((tm, LANES), lambda i, j: (i, 0)),
        ],
        out_shape=[jax.ShapeDtypeStruct((T, N), BF16), jax.ShapeDtypeStruct((T, LANES), F32)],
        scratch_shapes=[pltpu.VMEM((tm, D), BF16)],
        compiler_params=_cparams(("parallel", "arbitrary")),
        name="proj",
    )(x2, gain, w_main, w_ab)


def _qkvprep_kernel(xm_ref, xp_ref, xn_ref, cw_ref, o_ref, *, n_norm):
    i = pl.program_id(1)
    last = pl.num_programs(1) - 1
    ts = xm_ref.shape[1]
    hal = xp_ref.shape[1]
    keep_p = (i > 0).astype(F32)
    keep_n = (i < last).astype(F32)
    ext = ts + 2 * hal
    mid = slice(hal, hal + ts)
    for c in range(xm_ref.shape[2] // LANES):
        sl = slice(c * LANES, (c + 1) * LANES)
        x = xm_ref[0, :, sl].astype(F32)
        xe = jnp.concatenate([xp_ref[0, :, sl].astype(F32) * keep_p, x, xn_ref[0, :, sl].astype(F32) * keep_n], axis=0)
        w = cw_ref[:, sl]
        hy = (w[0:1] * pltpu.roll(xe, 2, 0)[mid] + w[1:2] * pltpu.roll(xe, 1, 0)[mid] + w[2:3] * x
              + w[3:4] * pltpu.roll(xe, ext - 1, 0)[mid])
        y = hy + hy * jnp.tanh(hy)
        if c < n_norm:
            y = y * lax.rsqrt(jnp.sum(y * y, axis=-1, keepdims=True) + EPS)
        o_ref[0, :, sl] = y.astype(o_ref.dtype)


def _qkvprep(P3, conv_w, ts):
    Bn, S, _ = P3.shape
    C = conv_w.shape[1]
    hal = BF16_ROWS
    nb = ts // hal
    return pl.pallas_call(
        functools.partial(_qkvprep_kernel, n_norm=2 * GDN_HEADS),
        grid=(Bn, S // ts),
        in_specs=[
            pl.BlockSpec((1, ts, C), lambda b, i: (b, i, 0)),
            pl.BlockSpec((1, hal, C), lambda b, i: (b, jnp.maximum(i * nb - 1, 0), 0)),
            pl.BlockSpec((1, hal, C), lambda b, i: (b, jnp.minimum((i + 1) * nb, S // hal - 1), 0)),
            _const_spec(conv_w.shape),
        ],
        out_specs=pl.BlockSpec((1, ts, C), lambda b, i: (b, i, 0)),
        out_shape=jax.ShapeDtypeStruct((Bn, S, C), BF16),
        compiler_params=_cparams(("parallel", "parallel")),
        name="gdn_prep",
    )(P3, P3, P3, conv_w)


def _mm_split(lhs, w, n_pass):
    lh, wh = lhs.astype(BF16), w.astype(BF16)
    if n_pass == 1:
        return _dot(lh, wh)
    ll = (lhs - lh.astype(F32)).astype(BF16)
    wl = (w - wh.astype(F32)).astype(BF16)
    return _dot(lh, wh) + (_dot(lh, wl) + _dot(ll, wh))


def _gdn_prepare(d, qc, gam, beta, grow, msk):
    C = GDN_CHUNK
    H = GDN_HEADS
    Dh = LANES
    blk, incl, strict, eye = msk
    stack = lambda off: jnp.concatenate([qc[:, off + h * Dh: off + (h + 1) * Dh] for h in range(H)], axis=0)
    Qs = stack(0).astype(F32) * (Dh ** -0.5)
    Kb16 = stack(H * Dh)
    Ks = Kb16.astype(F32)
    Vs = stack(2 * H * Dh).astype(F32)
    gcol = jnp.concatenate(gam, axis=0)
    bcol = jnp.concatenate(beta, axis=0)
    edge = C - 1 if d == 0 else 0
    glast = [g[edge:edge + 1] for g in gam]
    glast_col = jnp.concatenate([jnp.broadcast_to(g, (C, 1)) for g in glast], axis=0)

    diff = _side_by_side(gam, blk) - grow
    decay = jnp.where(incl, jnp.exp(jnp.where(incl, diff, 0.0)), 0.0)
    kb = Ks * bcol
    g2 = _dot_nt(jnp.concatenate([kb, Qs], axis=0).astype(BF16), Kb16)
    kk = _side_by_side([g2[h * C:(h + 1) * C] for h in range(H)], blk)
    qk = _side_by_side([g2[(H + h) * C:(H + h + 1) * C] for h in range(H)], blk) * decay
    X = jnp.where(strict, -(kk * decay), 0.0)
    egam = jnp.exp(gcol)
    return dict(X=X, P=eye + X, qk=qk, rhs=jnp.concatenate([Vs * bcol, kb * egam], axis=1).astype(BF16),
                qdec=Qs * egam, kdec=Ks * jnp.exp(glast_col - gcol), eg=[jnp.exp(g) for g in glast])


def _side_by_side(cols, blk):
    out = jnp.where(blk[0], cols[0], 0.0)
    for h in range(1, len(blk)):
        out = jnp.where(blk[h], cols[h], out)
    return out


def _bd(m, blk):
    return jnp.concatenate([jnp.where(b, m, 0.0) for b in blk], axis=0)


def _gdn_inverse_step(c, k, blk):
    C = GDN_CHUNK
    w = _bd(c["X"], blk)
    if k == 0:
        c["X"] = _mm_split(c["X"], w, GDN_INV_PASSES)
    elif k < 5:
        r = _mm_split(jnp.concatenate([c["P"], c["X"]], axis=0), w, GDN_INV_PASSES)
        c["P"] = c["P"] + r[:C]
        c["X"] = r[C:]
    else:
        c["P"] = c["P"] + _mm_split(c["P"], w, GDN_INV_PASSES)


def _gdn_wy(c, blk):
    C = GDN_CHUNK
    H = GDN_HEADS
    wy = _dot(_bd(c["P"], blk).astype(BF16), c["rhs"])
    c["U"], Wm = wy[:, :LANES], wy[:, LANES:]
    qdec = c["qdec"]
    c["lhs"] = [jnp.concatenate([Wm[h * C:(h + 1) * C], qdec[h * C:(h + 1) * C]], axis=0).astype(BF16)
                for h in range(H)]
    c["kT"] = c["kdec"].T.astype(BF16)
    c["egrow"] = jnp.concatenate([jnp.broadcast_to(e, (1, LANES)) for e in c["eg"]], axis=1)
    c["qkbd"] = _bd(c["qk"], blk).astype(BF16)


def _gdn_kernel(qf_ref, af_ref, qb_ref, abk_ref, prm_ref, of_ref, ob_ref, sf_ref, sb_ref):
    C = GDN_CHUNK
    H = GDN_HEADS
    CT = qf_ref.shape[1]
    nch = CT // C

    @pl.when(pl.program_id(1) == 0)
    def _():
        sf_ref[...] = jnp.zeros_like(sf_ref)
        sb_ref[...] = jnp.zeros_like(sb_ref)

    ri = lax.broadcasted_iota(jnp.int32, (C, H * C), 0)
    lane = lax.broadcasted_iota(jnp.int32, (C, H * C), 1)
    cj = lane % C
    blk = [(lane // C) == h for h in range(H)]
    eye = (cj == ri).astype(F32)
    HC = H * C
    rt = lax.broadcasted_iota(jnp.int32, (HC, HC), 0)
    ct = lax.broadcasted_iota(jnp.int32, (HC, HC), 1)
    same_t = (rt // C) == (ct // C)
    neg_a = -jnp.exp(prm_ref[0:1, :])
    dt_b = prm_ref[1:2, :]

    dirs = ((0, qf_ref, af_ref, of_ref, sf_ref), (1, qb_ref, abk_ref, ob_ref, sb_ref))

    chunks = {}
    for d, q_ref, a_ref, o_ref, s_ref in dirs:
        ab = a_ref[0]
        bt = _sigmoid(ab)
        la = neg_a * _softplus(ab + dt_b)
        cum = (same_t & ((ct <= rt) if d == 0 else (ct >= rt))).astype(F32)
        gam = jnp.concatenate([jnp.dot(cum, la[r0:r0 + HC], precision=HI, preferred_element_type=F32)
                               for r0 in range(0, CT, HC)], axis=0)
        slab = gam.T[2 * H:4 * H, :]
        msk = (blk, (cj <= ri) if d == 0 else (cj >= ri), (cj < ri) if d == 0 else (cj > ri), eye)
        for ci in range(nch):
            rows = slice(ci * C, (ci + 1) * C)
            gl = [gam[rows, 2 * H + d * H + h: 2 * H + d * H + h + 1] for h in range(H)]
            bl = [bt[rows, d * H + h: d * H + h + 1] for h in range(H)]
            pieces = []
            for h in range(H):
                shift = ((h - ci) * C) % CT
                moved = slab if shift == 0 else pltpu.roll(slab, shift, 1)
                pieces.append(moved[d * H + h:d * H + h + 1, :HC])
            grow = _side_by_side(pieces, [b[0:1, :] for b in blk])
            chunks[d, ci] = _gdn_prepare(d, q_ref[0, rows, :], gl, bl, grow, msk)

    order = [(d, step if d == 0 else nch - 1 - step) for step in range(nch) for d in (0, 1)]
    for k in range(6):
        for key in order:
            _gdn_inverse_step(chunks[key], k, blk)
    for key in order:
        _gdn_wy(chunks[key], blk)

    state = {d: jnp.concatenate([s_ref[h] for h in range(H)], axis=1) for d, _, _, _, s_ref in dirs}
    zero = jnp.zeros((C, LANES), BF16)
    for step in range(nch):
        cur = {d: chunks[d, step if d == 0 else nch - 1 - step] for d in (0, 1)}
        r = {d: [_dot(cur[d]["lhs"][h], state[d][:, h * LANES:(h + 1) * LANES].astype(BF16)) for h in range(H)]
             for d in (0, 1)}
        vbd = {}
        for d, _, _, o_ref, _ in dirs:
            c = cur[d]
            vnew = [(c["U"][h * C:(h + 1) * C] - r[d][h][:C]).astype(BF16) for h in range(H)]
            out = jnp.concatenate([r[d][h][C:] for h in range(H)], axis=0) + _dot(c["qkbd"], jnp.concatenate(vnew, axis=0))
            ci = step if d == 0 else nch - 1 - step
            for h in range(H):
                o_ref[0, ci * C:(ci + 1) * C, h * LANES:(h + 1) * LANES] = out[h * C:(h + 1) * C].astype(o_ref.dtype)
            vbd[d] = jnp.concatenate([jnp.concatenate([vnew[h] if g == h else zero for g in range(H)], axis=1)
                                      for h in range(H)], axis=0)
        for d in (0, 1):
            state[d] = state[d] * cur[d]["egrow"] + _dot(cur[d]["kT"], vbd[d])
    for d, _, _, _, s_ref in dirs:
        for h in range(H):
            s_ref[h] = state[d][:, h * LANES:(h + 1) * LANES]


def _gdn(qkv, ab3, prm, ct, C3):
    Bn, S, _ = qkv.shape
    W = C3 // 3
    nt = S // ct
    dh = W // GDN_HEADS
    assert ct % (GDN_HEADS * GDN_CHUNK) == 0 and dh == LANES
    fwd = lambda b, n: (b, n, 0)
    bwd = lambda b, n: (b, nt - 1 - n, 0)
    return pl.pallas_call(
        _gdn_kernel,
        grid=(Bn, nt),
        in_specs=[
            pl.BlockSpec((1, ct, C3), fwd),
            pl.BlockSpec((1, ct, LANES), fwd),
            pl.BlockSpec((1, ct, C3), bwd),
            pl.BlockSpec((1, ct, LANES), bwd),
            _const_spec(prm.shape),
        ],
        out_specs=[pl.BlockSpec((1, ct, W), fwd), pl.BlockSpec((1, ct, W), bwd)],
        out_shape=[jax.ShapeDtypeStruct((Bn, S, W), BF16)] * 2,
        scratch_shapes=[pltpu.VMEM((GDN_HEADS, dh, dh), F32)] * 2,
        compiler_params=_cparams(("parallel", "arbitrary")),
        name="gdn",
    )(qkv, ab3, qkv, ab3, prm)


def _s5_kernel(uf_ref, ub_ref, bw_ref, cw_ref, a_ref, yf_ref, yb_ref, bu_ref, hb_ref, carry_ref, *, nb):
    R = uf_ref.shape[0]
    tt = R // nb
    nq = bw_ref.shape[1]
    ns = bw_ref.shape[3]
    half = ns // 2
    n_tile = 2 * LANES
    u_refs, y_refs = (uf_ref, ub_ref), (yf_ref, yb_ref)
    units = [(d, q) for d in (0, 1) for q in range(nq)]

    @pl.when(pl.program_id(0) == 0)
    def _():
        carry_ref[...] = jnp.zeros_like(carry_ref)

    def expand(k, n):
        d, q = units[k]
        cols = slice(n * n_tile, (n + 1) * n_tile)
        bu_ref[k % 2, :, cols] = _dot(u_refs[d][:, q * LANES:(q + 1) * LANES], bw_ref[d, q, :, cols])

    def contract(k, part, parts=2):
        d, q = units[k]
        rows = slice(part * R // parts, (part + 1) * R // parts)
        y_refs[d][rows, q * LANES:(q + 1) * LANES] = _dot(hb_ref[k % 2, rows, :], cw_ref[d, q]).astype(y_refs[d].dtype)

    for n in range(ns // n_tile):
        expand(0, n)
    for k, (d, q) in enumerate(units):
        mxu = []
        if k + 1 < len(units):
            mxu += [functools.partial(expand, k + 1, n) for n in range(ns // n_tile)]
        if k > 0:
            mxu += [functools.partial(contract, k - 1, p) for p in range(2)]
        every = -(-tt // len(mxu))
        ar = a_ref[d, 0, :, q * half:(q + 1) * half]
        ai = a_ref[d, 1, :, q * half:(q + 1) * half]
        cre = slice(q * ns, q * ns + half)
        cim = slice(q * ns + half, (q + 1) * ns)
        hr, hi = carry_ref[d, :, cre], carry_ref[d, :, cim]
        for s in range(tt):
            if mxu and s % every == 0:
                mxu.pop(0)()
            t = s if d == 0 else tt - 1 - s
            rows = slice(t * nb, (t + 1) * nb)
            hr, hi = (ar * hr - ai * hi + bu_ref[k % 2, rows, :half],
                      ar * hi + ai * hr + bu_ref[k % 2, rows, half:])
            hb_ref[k % 2, rows, :half] = hr.astype(BF16)
            hb_ref[k % 2, rows, half:] = hi.astype(BF16)
        for piece in mxu:
            piece()
        carry_ref[d, :, cre] = hr
        carry_ref[d, :, cim] = hi
    for p in range(2):
        contract(len(units) - 1, p)


def _s5(utm, bw, cw, a_b, nb, rows, col_blk):
    TB = utm.shape[0]
    nq = bw.shape[1]
    W = nq * LANES
    nstate = bw.shape[3] * nq
    nt = TB // rows
    fwd = lambda n: (n, col_blk)
    bwd = lambda n: (nt - 1 - n, col_blk)
    return pl.pallas_call(
        functools.partial(_s5_kernel, nb=nb),
        grid=(nt,),
        in_specs=[
            pl.BlockSpec((rows, W), fwd),
            pl.BlockSpec((rows, W), bwd),
            _const_spec(bw.shape),
            _const_spec(cw.shape),
            _const_spec(a_b.shape),
        ],
        out_specs=[pl.BlockSpec((rows, W), lambda n: (n, 0)), pl.BlockSpec((rows, W), lambda n: (nt - 1 - n, 0))],
        out_shape=[jax.ShapeDtypeStruct((TB, W), BF16)] * 2,
        scratch_shapes=[pltpu.VMEM((2, rows, bw.shape[3]), F32), pltpu.VMEM((2, rows, bw.shape[3]), BF16),
                        pltpu.VMEM((2, nb, nstate), F32)],
        compiler_params=_cparams(("arbitrary",)),
        name="s5_scan",
    )(utm, utm, bw, cw, a_b)


def _lru_kernel(xf_ref, xfp_ref, xfn_ref, xb_ref, xbp_ref, xbn_ref, cw_ref, cb_ref, wg_ref, bg_ref, lam_ref,
                of_ref, ob_ref, a_s, b_s, carry_ref, *, nb):
    n = pl.program_id(0)
    nt = pl.num_programs(0)
    R = xf_ref.shape[0]
    W = xf_ref.shape[1]
    tt = R // nb

    @pl.when(n == 0)
    def _():
        carry_ref[...] = jnp.zeros_like(carry_ref)

    streams = ((0, n, xf_ref, xfp_ref, xfn_ref, of_ref), (1, nt - 1 - n, xb_ref, xbp_ref, xbn_ref, ob_ref))
    for d, pos, xm_ref, xp_ref, xn_ref, o_ref in streams:
        xp = xp_ref[...].astype(F32) * (pos > 0).astype(F32)
        xn = xn_ref[...].astype(F32) * (pos < nt - 1).astype(F32)
        xe = jnp.concatenate([xp, xm_ref[...].astype(F32), xn], axis=0)
        w = cw_ref[...]
        xc = cb_ref[...] + sum(w[k:k + 1] * xe[k * nb:k * nb + R] for k in range(LRU_CONV))
        th = jnp.tanh(_dot(xc.astype(BF16), wg_ref[d]) + bg_ref[d])
        ig = 0.5 * th[:, W:] + 0.5
        c2 = (-0.5 * LRU_C) * _softplus(-lam_ref[d])
        a = jnp.exp(c2 * th[:, :W] + c2)
        a_s[d] = a
        b_s[d] = jnp.sqrt(1.0 - a * a) * (ig * xc)

        def body(s, h, d=d, o_ref=o_ref):
            t = s if d == 0 else tt - 1 - s
            rows = pl.ds(pl.multiple_of(t * nb, nb), nb)
            h = a_s[d, rows, :] * h + b_s[d, rows, :]
            o_ref[rows, :] = h.astype(o_ref.dtype)
            return h

        carry_ref[d] = lax.fori_loop(0, tt, body, carry_ref[d], unroll=4)


def _lru(utm, conv_w, conv_b, wg, bg, lam, nb, rows, col_blk):
    TB = utm.shape[0]
    W = conv_w.shape[1]
    nt = TB // rows
    hp = 2 * nb
    hn = nb
    nbt = TB // nb

    def specs(pos):
        return [
            pl.BlockSpec((rows, W), lambda n: (pos(n), col_blk)),
            pl.BlockSpec((hp, W), lambda n: (jnp.maximum(pos(n) * (rows // hp) - 1, 0), col_blk)),
            pl.BlockSpec((hn, W), lambda n: (jnp.minimum((pos(n) + 1) * (rows // hn), nbt - 1), col_blk)),
        ]

    fwd = lambda n: n
    bwd = lambda n: nt - 1 - n
    return pl.pallas_call(
        functools.partial(_lru_kernel, nb=nb),
        grid=(nt,),
        in_specs=specs(fwd) + specs(bwd) + [_const_spec(conv_w.shape), _const_spec(conv_b.shape),
                                            _const_spec(wg.shape), _const_spec(bg.shape), _const_spec(lam.shape)],
        out_specs=[pl.BlockSpec((rows, W), lambda n: (n, 0)), pl.BlockSpec((rows, W), lambda n: (nt - 1 - n, 0))],
        out_shape=[jax.ShapeDtypeStruct((TB, W), BF16)] * 2,
        scratch_shapes=[pltpu.VMEM((2, rows, W), F32), pltpu.VMEM((2, rows, W), F32), pltpu.VMEM((2, nb, W), F32)],
        compiler_params=_cparams(("arbitrary",)),
        name="lru_scan",
    )(utm, utm, utm, utm, utm, utm, conv_w, conv_b, wg, bg, lam)


def _post_kernel(u_ref, g_ref, yf_ref, yb_ref, hf_ref, hb_ref, d_ref, gw_ref, gb_ref, perm_ref, o_ref):
    nb, tt = o_ref.shape[0], o_ref.shape[1]
    y = u_ref[...].astype(F32) * d_ref[...] + yf_ref[...].astype(F32) + yb_ref[...].astype(F32)
    zg = jax.nn.gelu(y)
    s5 = zg * _sigmoid(_dot(zg.astype(BF16), gw_ref[...]) + gb_ref[...])
    h = hf_ref[...].astype(F32) + hb_ref[...].astype(F32)
    lru = h * jax.nn.gelu(g_ref[...].astype(F32))
    res = jnp.concatenate([s5, lru], axis=1).astype(BF16)
    tok = _dot(perm_ref[...], res).astype(o_ref.dtype)
    for b in range(nb):
        o_ref[b] = tok[b * tt:(b + 1) * tt]


def _post(utm, yf, yb, hf, hb, s5_d, glu_w, glu_b, nb, rows, u_blk, g_blk):
    TB, W = yf.shape
    tt = rows // nb
    r = jnp.arange(rows)
    perm = (r[None, :] == ((r % tt) * nb + r // tt)[:, None]).astype(BF16)
    row = lambda n: (n, 0)
    return pl.pallas_call(
        _post_kernel,
        grid=(TB // rows,),
        in_specs=[
            pl.BlockSpec((rows, W), lambda n: (n, u_blk)),
            pl.BlockSpec((rows, W), lambda n: (n, g_blk)),
            pl.BlockSpec((rows, W), row), pl.BlockSpec((rows, W), row),
            pl.BlockSpec((rows, W), row), pl.BlockSpec((rows, W), row),
            _const_spec(s5_d.shape), _const_spec(glu_w.shape), _const_spec(glu_b.shape), _const_spec(perm.shape),
        ],
        out_specs=pl.BlockSpec((nb, tt, 2 * W), lambda n: (0, n, 0)),
        out_shape=jax.ShapeDtypeStruct((nb, TB // nb, 2 * W), BF16),
        compiler_params=_cparams(("parallel",)),
        name="s5_lru_post",
    )(utm, utm, yf, yb, hf, hb, s5_d, glu_w, glu_b, perm)


def _merge_kernel(x_ref, of_ref, ob_ref, z_ref, sl_ref, scb_ref, scc_ref, scx_ref, ccp_ref, cxp_ref, ccn_ref,
                  cxn_ref, g0_ref, g1_ref, g2_ref, g3_ref, gn_ref, scw_ref, wb_ref, wo_ref, o_ref):
    i = pl.program_id(1)
    last = pl.num_programs(1) - 1
    tm = x_ref.shape[1]
    W = z_ref.shape[2]
    hal = ccp_ref.shape[1]

    cx = scc_ref[0].astype(F32) * scx_ref[0].astype(F32)
    cx_p = (ccp_ref[0, hal - 1:hal].astype(F32) * cxp_ref[0, hal - 1:hal].astype(F32)) * (i > 0).astype(F32)
    cx_n = (ccn_ref[0, 0:1].astype(F32) * cxn_ref[0, 0:1].astype(F32)) * (i < last).astype(F32)
    row = lax.broadcasted_iota(jnp.int32, (tm, 1), 0)
    m1 = jnp.where(row == 0, cx_p, pltpu.roll(cx, 1, 0))
    p1 = jnp.where(row == tm - 1, cx_n, pltpu.roll(cx, tm - 1, 0))
    w = scw_ref[...]
    conv = w[0:1] * m1 + w[1:2] * cx + w[2:3] * p1
    g_refs = (g0_ref, g1_ref, g2_ref, g3_ref)

    def gated_sum(rs):
        o = of_ref[0, rs, :].astype(F32) + ob_ref[0, rs, :].astype(F32)
        z = z_ref[0, rs, :].astype(F32)
        ys = []
        for h in range(GDN_HEADS):
            oh = o[:, h * LANES:(h + 1) * LANES]
            ys.append(oh * lax.rsqrt(jnp.mean(oh * oh, axis=-1, keepdims=True) + EPS) * gn_ref[...])
        y_gdn = jnp.concatenate(ys, axis=1) * (z * _sigmoid(z))
        y_sc = scb_ref[0, rs, :].astype(F32) * conv[rs]
        branches = (y_gdn.astype(BF16), sl_ref[0, rs, :W], sl_ref[0, rs, W:], y_sc.astype(BF16))
        merged = None
        for m, y in enumerate(branches):
            hd = _dot(y, wb_ref[m])
            t = hd + hd * jnp.tanh(g_refs[m][0, rs, :].astype(F32))
            merged = t if merged is None else merged + t
        return merged.astype(BF16)

    half = tm // 2
    rows = [slice(0, half), slice(half, tm)]
    merged = gated_sum(rows[0])
    for k in range(2):
        nxt = gated_sum(rows[k + 1]) if k == 0 else None
        o_ref[0, rows[k], :] = x_ref[0, rows[k], :] + _dot(merged, wo_ref[...])
        merged = nxt


def _merge(x3, o_f, o_b, P3, sl3, gdn_gain, sc_w, w_branch, w_out, tm, blk):
    Bn, S, D = x3.shape
    W = o_f.shape[2]
    hal = BF16_ROWS
    nb = tm // hal
    tok = lambda b, i: (b, i, 0)
    col = lambda k: (lambda b, i: (b, i, k))
    prev = lambda k: (lambda b, i: (b, jnp.maximum(i * nb - 1, 0), k))
    nxt = lambda k: (lambda b, i: (b, jnp.minimum((i + 1) * nb, S // hal - 1), k))
    gate0 = blk["gates"] * W // D
    return pl.pallas_call(
        _merge_kernel,
        grid=(Bn, S // tm),
        in_specs=[
            pl.BlockSpec((1, tm, D), tok),
            pl.BlockSpec((1, tm, W), tok), pl.BlockSpec((1, tm, W), tok),
            pl.BlockSpec((1, tm, W), col(blk["z"])),
            pl.BlockSpec((1, tm, 2 * W), tok),
            pl.BlockSpec((1, tm, W), col(blk["sc_b"])),
            pl.BlockSpec((1, tm, W), col(blk["sc_c"])),
            pl.BlockSpec((1, tm, W), col(blk["sc_x"])),
            pl.BlockSpec((1, hal, W), prev(blk["sc_c"])), pl.BlockSpec((1, hal, W), prev(blk["sc_x"])),
            pl.BlockSpec((1, hal, W), nxt(blk["sc_c"])), pl.BlockSpec((1, hal, W), nxt(blk["sc_x"])),
            pl.BlockSpec((1, tm, D), col(gate0)), pl.BlockSpec((1, tm, D), col(gate0 + 1)),
            pl.BlockSpec((1, tm, D), col(gate0 + 2)), pl.BlockSpec((1, tm, D), col(gate0 + 3)),
            _const_spec(gdn_gain.shape), _const_spec(sc_w.shape), _const_spec(w_branch.shape),
            _const_spec(w_out.shape),
        ],
        out_specs=pl.BlockSpec((1, tm, D), tok),
        out_shape=jax.ShapeDtypeStruct((Bn, S, D), F32),
        compiler_params=_cparams(("parallel", "parallel")),
        name="merge",
    )(x3, o_f, o_b, P3, sl3, P3, P3, P3, P3, P3, P3, P3, P3, P3, P3, P3, gdn_gain, sc_w, w_branch, w_out)


def _kv_kernel(m_ref, g_ref, w_ref, o_ref):
    o_ref[0] = _dot(_rms(m_ref[0], g_ref[...]).astype(BF16), w_ref[...]).astype(o_ref.dtype)


def _kv(mem, gain, w_kv):
    Bn, M, D = mem.shape
    return pl.pallas_call(
        _kv_kernel,
        grid=(Bn,),
        in_specs=[pl.BlockSpec((1, M, D), lambda b: (b, 0, 0)), _const_spec(gain.shape), _const_spec(w_kv.shape)],
        out_specs=pl.BlockSpec((1, M, 2 * D), lambda b: (b, 0, 0)),
        out_shape=jax.ShapeDtypeStruct((Bn, M, 2 * D), BF16),
        compiler_params=_cparams(("parallel",)),
        name="xa_kv",
    )(mem, gain, w_kv)


def _xa_kernel(x_ref, kv_ref, g_ref, wq_ref, wo_ref, o_ref):
    D = x_ref.shape[2]
    dh = D // XA_HEADS
    x = x_ref[0]
    q = _dot(_rms(x, g_ref[...]).astype(BF16), wq_ref[...]).astype(BF16)
    def scores(h):
        return _dot_nt(q[:, h * dh:(h + 1) * dh], kv_ref[0, :, h * dh:(h + 1) * dh]) * (dh ** -0.5)

    def softmax(s):
        e = jnp.exp(s - jnp.max(s, axis=-1, keepdims=True))
        return (e / jnp.sum(e, axis=-1, keepdims=True)).astype(BF16)

    def values(h, p):
        return _dot(p, kv_ref[0, :, D + h * dh:D + (h + 1) * dh])

    H = XA_HEADS
    s, p, outs = {}, {}, {}
    s[0] = scores(0)
    for h in range(H + 2):
        if h + 1 < H:
            s[h + 1] = scores(h + 1)
        if h < H:
            p[h] = softmax(s[h])
        if 1 <= h <= H:
            outs[h - 1] = values(h - 1, p[h - 1])
    o = jnp.concatenate([outs[h] for h in range(H)], axis=1).astype(BF16)
    o_ref[0] = x + _dot(o, wo_ref[...])


def _xa(x3, kv, gain, wq, wo, tm):
    Bn, S, D = x3.shape
    M = kv.shape[1]
    tok = lambda b, i: (b, i, 0)
    return pl.pallas_call(
        _xa_kernel,
        grid=(Bn, S // tm),
        in_specs=[pl.BlockSpec((1, tm, D), tok), pl.BlockSpec((1, M, 2 * D), lambda b, i: (b, 0, 0)),
                  _const_spec(gain.shape), _const_spec(wq.shape), _const_spec(wo.shape)],
        out_specs=pl.BlockSpec((1, tm, D), tok),
        out_shape=jax.ShapeDtypeStruct((Bn, S, D), F32),
        compiler_params=_cparams(("parallel", "parallel")),
        name="xattn",
    )(x3, kv, gain, wq, wo)


def _ffn_kernel(x_ref, xp_ref, xn_ref, g_ref, wu_ref, cw_ref, cb_ref, wd_ref, *rest, tf):
    fg_ref = rest[0] if len(rest) == 3 else None
    o_ref, act_ref = rest[-2:]
    i = pl.program_id(1)
    last = pl.num_programs(1) - 1
    tm = x_ref.shape[1]
    hal = xp_ref.shape[1]
    F = wd_ref.shape[0]
    x = x_ref[0]
    g = g_ref[...]
    hp = _rms(xp_ref[0], g) * (i > 0).astype(F32)
    hn = _rms(xn_ref[0], g) * (i < last).astype(F32)
    he = jnp.concatenate([hp, _rms(x, g), hn], axis=0).astype(BF16)
    ext = tm + 2 * hal

    def conv(u, cols):
        w = cw_ref[:, cols]
        m1 = pltpu.roll(u, 1, 0)[hal:hal + tm]
        p1 = pltpu.roll(u, ext - 1, 0)[hal:hal + tm]
        return w[0:1] * m1 + w[1:2] * u[hal:hal + tm] + w[2:3] * p1 + cb_ref[:, cols]

    for f in range(F // tf):
        gc = slice(f * tf, (f + 1) * tf)
        uc = slice(F + f * tf, F + (f + 1) * tf)
        gate = conv(_dot(he, wu_ref[:, gc]), gc)
        up = conv(_dot(he, wu_ref[:, uc]), uc)
        act_ref[:, gc] = (gate * _sigmoid(gate) * up).astype(BF16)
    out = x + _dot(act_ref[...], wd_ref[...])
    o_ref[0] = out if fg_ref is None else _rms(out, fg_ref[...])


def _ffn(x3, gain, w_up, conv_w, conv_b, w_down, tm, tf, final_gain=None):
    Bn, S, D = x3.shape
    hal = 8
    nb = tm // hal
    tok = lambda b, i: (b, i, 0)
    resident = lambda shape: pl.BlockSpec(shape, lambda b, i: (0, 0), pipeline_mode=pl.Buffered(1))
    return pl.pallas_call(
        functools.partial(_ffn_kernel, tf=tf),
        grid=(Bn, S // tm),
        in_specs=[
            pl.BlockSpec((1, tm, D), tok),
            pl.BlockSpec((1, hal, D), lambda b, i: (b, jnp.maximum(i * nb - 1, 0), 0)),
            pl.BlockSpec((1, hal, D), lambda b, i: (b, jnp.minimum((i + 1) * nb, S // hal - 1), 0)),
            _const_spec(gain.shape), resident(w_up.shape), _const_spec(conv_w.shape), _const_spec(conv_b.shape),
            resident(w_down.shape),
        ] + ([] if final_gain is None else [_const_spec(final_gain.shape)]),
        out_specs=pl.BlockSpec((1, tm, D), tok),
        out_shape=jax.ShapeDtypeStruct((Bn, S, D), F32),
        scratch_shapes=[pltpu.VMEM((tm, w_down.shape[0]), BF16)],
        compiler_params=_cparams(("parallel", "parallel")),
        name="ffn",
    )(x3, x3, x3, gain, w_up, conv_w, conv_b, w_down, *([] if final_gain is None else [final_gain]))


def _s5_weights(lam_re, lam_im, log_step, b_re, b_im, c_re, c_im, nb):
    G, P, J = b_re.shape[1:]
    gq = LANES // J
    nq = G // gq
    lr = jnp.minimum(lam_re, -1e-4)
    li = lam_im
    step = jnp.exp(log_step)[..., None]
    mag = jnp.exp(lr * step)
    a_re, a_im = mag * jnp.cos(li * step), mag * jnp.sin(li * step)
    den = lr * lr + li * li
    f_re = ((a_re - 1.0) * lr + a_im * li) / den
    f_im = (a_im * lr - (a_re - 1.0) * li) / den
    bb_re = f_re[..., None] * b_re - f_im[..., None] * b_im
    bb_im = f_re[..., None] * b_im + f_im[..., None] * b_re
    eye = jnp.eye(gq, dtype=F32)

    def expand(bb):
        t = bb.reshape(2, nq, gq, P, J)
        return jnp.einsum("dqgpj,gh->dqgjhp", t, eye).reshape(2, nq, gq * J, gq * P)

    def contract(cc):
        t = cc.reshape(2, nq, gq, J, P)
        return jnp.einsum("dqgjp,gh->dqgphj", t, eye).reshape(2, nq, gq * P, gq * J)

    bw = jnp.concatenate([expand(bb_re), expand(bb_im)], axis=3).astype(BF16)
    cw = jnp.concatenate([contract(c_re), -contract(c_im)], axis=2).astype(BF16)
    a_b = jnp.stack([a_re.reshape(2, G * P), a_im.reshape(2, G * P)], axis=1)
    a_b = jnp.broadcast_to(a_b[:, :, None, :], (2, 2, nb, G * P))
    return bw, cw, a_b


def _lru_gate_weights(wa, wx, ba, bx):
    nblk, bs = wa.shape[1], wa.shape[2]
    eye = jnp.eye(nblk, dtype=F32)
    dense = lambda w: jnp.einsum("dnkm,nl->dnklm", w, eye).reshape(2, nblk * bs, nblk * bs)
    wg = (0.5 * jnp.concatenate([dense(wa), dense(wx)], axis=2)).astype(BF16)
    bg = 0.5 * jnp.concatenate([ba, bx], axis=1)[:, None, :]
    return wg, bg


def _pick(n, pref):
    return pref if n % pref == 0 else n


def kernel(x, mem, mix_norm, w_in, gdn_conv, gdn_a_log, gdn_dt_bias, gdn_out_norm, s5_lambda_re, s5_lambda_im, s5_log_step, s5_b_re, s5_b_im, s5_c_re, s5_c_im, s5_d, s5_glu_w, s5_glu_b, lru_conv_w, lru_conv_b, lru_gate_a_w, lru_gate_a_b, lru_gate_x_w, lru_gate_x_b, lru_lambda, sc_conv, w_branch, w_mix_out, xa_norm, xa_mem_norm, xa_w_q, xa_w_kv, xa_w_o, ffn_norm, ffn_w_up, ffn_conv_w, ffn_conv_b, ffn_w_down, final_norm):
    Bn, S, D = x.shape
    depth = w_in.shape[0]
    W = D // 2
    H = GDN_HEADS
    T = Bn * S
    F = ffn_w_down.shape[1]
    assert Bn % BF16_ROWS == 0 and S % 256 == 0 and W == H * LANES

    blk = {"z": 3, "s5_u": 4, "lru_x": 5, "lru_g": 6, "sc_b": 7, "sc_c": 8, "sc_x": 9, "gates": 10}
    n_ab = 4 * H
    c_ab = 4 * W
    tm_proj = _pick(S, 1024)
    ts_tok = _pick(S, 512)
    ts_big = _pick(S, 1024)
    ct_gdn = _pick(S, 2 * GDN_HEADS * GDN_CHUNK)
    rows_post = 32 * Bn
    rows_scan = (64 if S % 64 == 0 else 32) * Bn

    row2 = lambda v: v.reshape(1, -1)
    for l in range(depth):
        c_gate = c_ab + n_ab + 6 * W
        w_main = jnp.concatenate([w_in[l][:, :c_ab], w_in[l][:, c_ab + n_ab:c_gate], 0.5 * w_in[l][:, c_gate:]],
                                 axis=1).astype(BF16)
        w_ab = jnp.pad(w_in[l][:, c_ab:c_ab + n_ab], ((0, 0), (0, LANES - n_ab))).astype(BF16)
        P, ab = _proj(x.reshape(T, D), row2(mix_norm[l]), w_main, w_ab, tm_proj, _pick(w_main.shape[1], 3072))
        P3 = P.reshape(Bn, S, -1)

        qkv = _qkvprep(P3, 0.5 * gdn_conv[l], ts_tok)
        prm = jnp.zeros((8, LANES), F32)
        prm = prm.at[0, 2 * H:4 * H].set(gdn_a_log[l].reshape(-1)).at[1, 2 * H:4 * H].set(gdn_dt_bias[l].reshape(-1))
        o_f, o_b = _gdn(qkv, ab.reshape(Bn, S, LANES), prm, ct_gdn, 3 * W)

        utm = jnp.swapaxes(P3[:, :, blk["s5_u"] * W:(blk["lru_g"] + 1) * W], 0, 1).reshape(S * Bn, 3 * W)
        bw, cw, a_b = _s5_weights(s5_lambda_re[l], s5_lambda_im[l], s5_log_step[l], s5_b_re[l], s5_b_im[l],
                                  s5_c_re[l], s5_c_im[l], Bn)
        y_f, y_b = _s5(utm, bw, cw, a_b, Bn, rows_scan, 0)
        wg, bg = _lru_gate_weights(lru_gate_a_w[l], lru_gate_x_w[l], lru_gate_a_b[l], lru_gate_x_b[l])
        h_f, h_b = _lru(utm, lru_conv_w[l], row2(lru_conv_b[l]), wg, bg, lru_lambda[l][:, None, :], Bn, rows_scan, 1)
        sl3 = _post(utm, y_f, y_b, h_f, h_b, row2(s5_d[l]), s5_glu_w[l].astype(BF16), row2(s5_glu_b[l]),
                    Bn, rows_post, 0, 2)

        x = _merge(x, o_f, o_b, P3, sl3, row2(gdn_out_norm[l]), sc_conv[l], (0.5 * w_branch[l]).astype(BF16),
                   w_mix_out[l].astype(BF16), ts_tok, blk)

        kv = _kv(mem, row2(xa_mem_norm[l]), xa_w_kv[l].astype(BF16))
        x = _xa(x, kv, row2(xa_norm[l]), xa_w_q[l].astype(BF16), xa_w_o[l].astype(BF16), ts_big)

        x = _ffn(x, row2(ffn_norm[l]), ffn_w_up[l].astype(BF16), ffn_conv_w[l], row2(ffn_conv_b[l]),
                 ffn_w_down[l].astype(BF16), ts_big, 256, row2(final_norm) if l == depth - 1 else None)
    return x
```

```python
import functools
import math

import jax
import jax.numpy as jnp
from jax import lax
from jax.experimental import pallas as pl
from jax.experimental.pallas import tpu as pltpu

F32 = jnp.float32
BF16 = jnp.bfloat16
EPS = 1e-6

LANES = 128
BF16_ROWS = 16
VMEM_LIMIT = 56 * 1024 * 1024

GDN_HEADS = 4
GDN_CHUNK = 64
GDN_CONV = 4
GDN_INV_PASSES = 1
LRU_CONV = 4
S5_GROUP = 16
S5_STATE = 64
LRU_C = 8.0
XA_HEADS = 4

HI = lax.Precision.HIGHEST


def _cparams(sem):
    return pltpu.CompilerParams(dimension_semantics=sem, vmem_limit_bytes=VMEM_LIMIT)


def _rms(x, g):
    return x * lax.rsqrt(jnp.mean(x * x, axis=-1, keepdims=True) + EPS) * g


def _dot(a, b):
    return jnp.dot(a, b, preferred_element_type=F32)


def _dot_nt(a, b):
    return lax.dot_general(a, b, (((1,), (1,)), ((), ())), preferred_element_type=F32)


def _sigmoid(x):
    return 0.5 * jnp.tanh(0.5 * x) + 0.5


def _softplus(x):
    return jnp.maximum(x, 0.0) + jnp.log(1.0 + jnp.exp(-jnp.abs(x)))


def _const_spec(shape):
    nd = len(shape)
    return pl.BlockSpec(shape, lambda *_: (0,) * nd)


def _proj_kernel(x_ref, g_ref, w_ref, wab_ref, p_ref, ab_ref, h_ref):
    @pl.when(pl.program_id(1) == 0)
    def _():
        hb = _rms(x_ref[...], g_ref[...]).astype(BF16)
        h_ref[...] = hb
        ab_ref[...] = _dot(hb, wab_ref[...])

    p_ref[...] = _dot(h_ref[...], w_ref[...]).astype(BF16)


def _proj(x2, gain, w_main, w_ab, tm, tn):
    T, D = x2.shape
    N = w_main.shape[1]
    return pl.pallas_call(
        _proj_kernel,
        grid=(T // tm, N // tn),
        in_specs=[
            pl.BlockSpec((tm, D), lambda i, j: (i, 0)),
            pl.BlockSpec((1, D), lambda i, j: (0, 0)),
            pl.BlockSpec((D, tn), lambda i, j: (0, j)),
            pl.BlockSpec((D, LANES), lambda i, j: (0, 0)),
        ],
        out_specs=[
            pl.BlockSpec((tm, tn), lambda i, j: (i, j)),
            pl.BlockSpec((tm, LANES), lambda i, j: (i, 0)),
        ],
        out_shape=[jax.ShapeDtypeStruct((T, N), BF16), jax.ShapeDtypeStruct((T, LANES), F32)],
        scratch_shapes=[pltpu.VMEM((tm, D), BF16)],
        compiler_params=_cparams(("parallel", "arbitrary")),
        name="proj",
    )(x2, gain, w_main, w_ab)


def _qkvprep_kernel(xm_ref, xp_ref, xn_ref, cw_ref, o_ref, *, n_norm):
    i = pl.program_id(1)
    last = pl.num_programs(1) - 1
    ts = xm_ref.shape[1]
    hal = xp_ref.shape[1]
    keep_p = (i > 0).astype(F32)
    keep_n = (i < last).astype(F32)
    ext = ts + 2 * hal
    mid = slice(hal, hal + ts)
    for c in range(xm_ref.shape[2] // LANES):
        sl = slice(c * LANES, (c + 1) * LANES)
        x = xm_ref[0, :, sl].astype(F32)
        xe = jnp.concatenate([xp_ref[0, :, sl].astype(F32) * keep_p, x, xn_ref[0, :, sl].astype(F32) * keep_n], axis=0)
        w = cw_ref[:, sl]
        hy = (w[0:1] * pltpu.roll(xe, 2, 0)[mid] + w[1:2] * pltpu.roll(xe, 1, 0)[mid] + w[2:3] * x
              + w[3:4] * pltpu.roll(xe, ext - 1, 0)[mid])
        y = hy + hy * jnp.tanh(hy)
        if c < n_norm:
            y = y * lax.rsqrt(jnp.sum(y * y, axis=-1, keepdims=True) + EPS)
        o_ref[0, :, sl] = y.astype(o_ref.dtype)


def _qkvprep(P3, conv_w, ts):
    Bn, S, _ = P3.shape
    C = conv_w.shape[1]
    hal = BF16_ROWS
    nb = ts // hal
    return pl.pallas_call(
        functools.partial(_qkvprep_kernel, n_norm=2 * GDN_HEADS),
        grid=(Bn, S // ts),
        in_specs=[
            pl.BlockSpec((1, ts, C), lambda b, i: (b, i, 0)),
            pl.BlockSpec((1, hal, C), lambda b, i: (b, jnp.maximum(i * nb - 1, 0), 0)),
            pl.BlockSpec((1, hal, C), lambda b, i: (b, jnp.minimum((i + 1) * nb, S // hal - 1), 0)),
            _const_spec(conv_w.shape),
        ],
        out_specs=pl.BlockSpec((1, ts, C), lambda b, i: (b, i, 0)),
        out_shape=jax.ShapeDtypeStruct((Bn, S, C), BF16),
        compiler_params=_cparams(("parallel", "parallel")),
        name="gdn_prep",
    )(P3, P3, P3, conv_w)


def _mm_split(lhs, w, n_pass):
    lh, wh = lhs.astype(BF16), w.astype(BF16)
    if n_pass == 1:
        return _dot(lh, wh)
    ll = (lhs - lh.astype(F32)).astype(BF16)
    wl = (w - wh.astype(F32)).astype(BF16)
    return _dot(lh, wh) + (_dot(lh, wl) + _dot(ll, wh))


def _gdn_prepare(d, qc, gam, beta, grow, msk):
    C = GDN_CHUNK
    H = GDN_HEADS
    Dh = LANES
    blk, incl, strict, eye = msk
    stack = lambda off: jnp.concatenate([qc[:, off + h * Dh: off + (h + 1) * Dh] for h in range(H)], axis=0)
    Qs = stack(0).astype(F32) * (Dh ** -0.5)
    Kb16 = stack(H * Dh)
    Ks = Kb16.astype(F32)
    Vs = stack(2 * H * Dh).astype(F32)
    gcol = jnp.concatenate(gam, axis=0)
    bcol = jnp.concatenate(beta, axis=0)
    edge = C - 1 if d == 0 else 0
    glast = [g[edge:edge + 1] for g in gam]
    glast_col = jnp.concatenate([jnp.broadcast_to(g, (C, 1)) for g in glast], axis=0)

    diff = _side_by_side(gam, blk) - grow
    decay = jnp.where(incl, jnp.exp(jnp.where(incl, diff, 0.0)), 0.0)
    kb = Ks * bcol
    g2 = _dot_nt(jnp.concatenate([kb, Qs], axis=0).astype(BF16), Kb16)
    kk = _side_by_side([g2[h * C:(h + 1) * C] for h in range(H)], blk)
    qk = _side_by_side([g2[(H + h) * C:(H + h + 1) * C] for h in range(H)], blk) * decay
    X = jnp.where(strict, -(kk * decay), 0.0)
    egam = jnp.exp(gcol)
    return dict(X=X, P=eye + X, qk=qk, rhs=jnp.concatenate([Vs * bcol, kb * egam], axis=1).astype(BF16),
                qdec=Qs * egam, kdec=Ks * jnp.exp(glast_col - gcol), eg=[jnp.exp(g) for g in glast])


def _side_by_side(cols, blk):
    out = jnp.where(blk[0], cols[0], 0.0)
    for h in range(1, len(blk)):
        out = jnp.where(blk[h], cols[h], out)
    return out


def _bd(m, blk):
    return jnp.concatenate([jnp.where(b, m, 0.0) for b in blk], axis=0)


def _gdn_inverse_step(c, k, blk):
    C = GDN_CHUNK
    w = _bd(c["X"], blk)
    if k == 0:
        c["X"] = _mm_split(c["X"], w, GDN_INV_PASSES)
    elif k < 5:
        r = _mm_split(jnp.concatenate([c["P"], c["X"]], axis=0), w, GDN_INV_PASSES)
        c["P"] = c["P"] + r[:C]
        c["X"] = r[C:]
    else:
        c["P"] = c["P"] + _mm_split(c["P"], w, GDN_INV_PASSES)


def _gdn_wy(c, blk):
    C = GDN_CHUNK
    H = GDN_HEADS
    wy = _dot(_bd(c["P"], blk).astype(BF16), c["rhs"])
    c["U"], Wm = wy[:, :LANES], wy[:, LANES:]
    qdec = c["qdec"]
    c["lhs"] = [jnp.concatenate([Wm[h * C:(h + 1) * C], qdec[h * C:(h + 1) * C]], axis=0).astype(BF16)
                for h in range(H)]
    c["kT"] = c["kdec"].T.astype(BF16)
    c["egrow"] = jnp.concatenate([jnp.broadcast_to(e, (1, LANES)) for e in c["eg"]], axis=1)
    c["qkbd"] = _bd(c["qk"], blk).astype(BF16)


def _gdn_kernel(qf_ref, af_ref, qb_ref, abk_ref, prm_ref, of_ref, ob_ref, sf_ref, sb_ref):
    C = GDN_CHUNK
    H = GDN_HEADS
    CT = qf_ref.shape[1]
    nch = CT // C

    @pl.when(pl.program_id(1) == 0)
    def _():
        sf_ref[...] = jnp.zeros_like(sf_ref)
        sb_ref[...] = jnp.zeros_like(sb_ref)

    ri = lax.broadcasted_iota(jnp.int32, (C, H * C), 0)
    lane = lax.broadcasted_iota(jnp.int32, (C, H * C), 1)
    cj = lane % C
    blk = [(lane // C) == h for h in range(H)]
    eye = (cj == ri).astype(F32)
    HC = H * C
    rt = lax.broadcasted_iota(jnp.int32, (HC, HC), 0)
    ct = lax.broadcasted_iota(jnp.int32, (HC, HC), 1)
    same_t = (rt // C) == (ct // C)
    neg_a = -jnp.exp(prm_ref[0:1, :])
    dt_b = prm_ref[1:2, :]

    dirs = ((0, qf_ref, af_ref, of_ref, sf_ref), (1, qb_ref, abk_ref, ob_ref, sb_ref))

    chunks = {}
    for d, q_ref, a_ref, o_ref, s_ref in dirs:
        ab = a_ref[0]
        bt = _sigmoid(ab)
        la = neg_a * _softplus(ab + dt_b)
        cum = (same_t & ((ct <= rt) if d == 0 else (ct >= rt))).astype(F32)
        gam = jnp.concatenate([jnp.dot(cum, la[r0:r0 + HC], precision=HI, preferred_element_type=F32)
                               for r0 in range(0, CT, HC)], axis=0)
        slab = gam.T[2 * H:4 * H, :]
        msk = (blk, (cj <= ri) if d == 0 else (cj >= ri), (cj < ri) if d == 0 else (cj > ri), eye)
        for ci in range(nch):
            rows = slice(ci * C, (ci + 1) * C)
            gl = [gam[rows, 2 * H + d * H + h: 2 * H + d * H + h + 1] for h in range(H)]
            bl = [bt[rows, d * H + h: d * H + h + 1] for h in range(H)]
            pieces = []
            for h in range(H):
                shift = ((h - ci) * C) % CT
                moved = slab if shift == 0 else pltpu.roll(slab, shift, 1)
                pieces.append(moved[d * H + h:d * H + h + 1, :HC])
            grow = _side_by_side(pieces, [b[0:1, :] for b in blk])
            chunks[d, ci] = _gdn_prepare(d, q_ref[0, rows, :], gl, bl, grow, msk)

    order = [(d, step if d == 0 else nch - 1 - step) for step in range(nch) for d in (0, 1)]
    for k in range(6):
        for key in order:
            _gdn_inverse_step(chunks[key], k, blk)
    for key in order:
        _gdn_wy(chunks[key], blk)

    state = {d: jnp.concatenate([s_ref[h] for h in range(H)], axis=1) for d, _, _, _, s_ref in dirs}
    zero = jnp.zeros((C, LANES), BF16)
    for step in range(nch):
        cur = {d: chunks[d, step if d == 0 else nch - 1 - step] for d in (0, 1)}
        r = {d: [_dot(cur[d]["lhs"][h], state[d][:, h * LANES:(h + 1) * LANES].astype(BF16)) for h in range(H)]
             for d in (0, 1)}
        vbd = {}
        for d, _, _, o_ref, _ in dirs:
            c = cur[d]
            vnew = [(c["U"][h * C:(h + 1) * C] - r[d][h][:C]).astype(BF16) for h in range(H)]
            out = jnp.concatenate([r[d][h][C:] for h in range(H)], axis=0) + _dot(c["qkbd"], jnp.concatenate(vnew, axis=0))
            ci = step if d == 0 else nch - 1 - step
            for h in range(H):
                o_ref[0, ci * C:(ci + 1) * C, h * LANES:(h + 1) * LANES] = out[h * C:(h + 1) * C].astype(o_ref.dtype)
            vbd[d] = jnp.concatenate([jnp.concatenate([vnew[h] if g == h else zero for g in range(H)], axis=1)
                                      for h in range(H)], axis=0)
        for d in (0, 1):
            state[d] = state[d] * cur[d]["egrow"] + _dot(cur[d]["kT"], vbd[d])
    for d, _, _, _, s_ref in dirs:
        for h in range(H):
            s_ref[h] = state[d][:, h * LANES:(h + 1) * LANES]


def _gdn(qkv, ab3, prm, ct, C3):
    Bn, S, _ = qkv.shape
    W = C3 // 3
    nt = S // ct
    dh = W // GDN_HEADS
    assert ct % (GDN_HEADS * GDN_CHUNK) == 0 and dh == LANES
    fwd = lambda b, n: (b, n, 0)
    bwd = lambda b, n: (b, nt - 1 - n, 0)
    return pl.pallas_call(
        _gdn_kernel,
        grid=(Bn, nt),
        in_specs=[
            pl.BlockSpec((1, ct, C3), fwd),
            pl.BlockSpec((1, ct, LANES), fwd),
            pl.BlockSpec((1, ct, C3), bwd),
            pl.BlockSpec((1, ct, LANES), bwd),
            _const_spec(prm.shape),
        ],
        out_specs=[pl.BlockSpec((1, ct, W), fwd), pl.BlockSpec((1, ct, W), bwd)],
        out_shape=[jax.ShapeDtypeStruct((Bn, S, W), BF16)] * 2,
        scratch_shapes=[pltpu.VMEM((GDN_HEADS, dh, dh), F32)] * 2,
        compiler_params=_cparams(("parallel", "arbitrary")),
        name="gdn",
    )(qkv, ab3, qkv, ab3, prm)


def _s5_kernel(uf_ref, ub_ref, bw_ref, cw_ref, a_ref, yf_ref, yb_ref, bu_ref, hb_ref, carry_ref, *, nb):
    R = uf_ref.shape[0]
    tt = R // nb
    nq = bw_ref.shape[1]
    ns = bw_ref.shape[3]
    half = ns // 2
    n_tile = 2 * LANES
    u_refs, y_refs = (uf_ref, ub_ref), (yf_ref, yb_ref)
    units = [(d, q) for d in (0, 1) for q in range(nq)]

    @pl.when(pl.program_id(0) == 0)
    def _():
        carry_ref[...] = jnp.zeros_like(carry_ref)

    def expand(k, n):
        d, q = units[k]
        cols = slice(n * n_tile, (n + 1) * n_tile)
        bu_ref[k % 2, :, cols] = _dot(u_refs[d][:, q * LANES:(q + 1) * LANES], bw_ref[d, q, :, cols])

    def contract(k, part, parts=2):
        d, q = units[k]
        rows = slice(part * R // parts, (part + 1) * R // parts)
        y_refs[d][rows, q * LANES:(q + 1) * LANES] = _dot(hb_ref[k % 2, rows, :], cw_ref[d, q]).astype(y_refs[d].dtype)

    for n in range(ns // n_tile):
        expand(0, n)
    for k, (d, q) in enumerate(units):
        mxu = []
        if k + 1 < len(units):
            mxu += [functools.partial(expand, k + 1, n) for n in range(ns // n_tile)]
        if k > 0:
            mxu += [functools.partial(contract, k - 1, p) for p in range(2)]
        every = -(-tt // len(mxu))
        ar = a_ref[d, 0, :, q * half:(q + 1) * half]
        ai = a_ref[d, 1, :, q * half:(q + 1) * half]
        cre = slice(q * ns, q * ns + half)
        cim = slice(q * ns + half, (q + 1) * ns)
        hr, hi = carry_ref[d, :, cre], carry_ref[d, :, cim]
        for s in range(tt):
            if mxu and s % every == 0:
                mxu.pop(0)()
            t = s if d == 0 else tt - 1 - s
            rows = slice(t * nb, (t + 1) * nb)
            hr, hi = (ar * hr - ai * hi + bu_ref[k % 2, rows, :half],
                      ar * hi + ai * hr + bu_ref[k % 2, rows, half:])
            hb_ref[k % 2, rows, :half] = hr.astype(BF16)
            hb_ref[k % 2, rows, half:] = hi.astype(BF16)
        for piece in mxu:
            piece()
        carry_ref[d, :, cre] = hr
        carry_ref[d, :, cim] = hi
    for p in range(2):
        contract(len(units) - 1, p)


def _s5(utm, bw, cw, a_b, nb, rows, col_blk):
    TB = utm.shape[0]
    nq = bw.shape[1]
    W = nq * LANES
    nstate = bw.shape[3] * nq
    nt = TB // rows
    fwd = lambda n: (n, col_blk)
    bwd = lambda n: (nt - 1 - n, col_blk)
    return pl.pallas_call(
        functools.partial(_s5_kernel, nb=nb),
        grid=(nt,),
        in_specs=[
            pl.BlockSpec((rows, W), fwd),
            pl.BlockSpec((rows, W), bwd),
            _const_spec(bw.shape),
            _const_spec(cw.shape),
            _const_spec(a_b.shape),
        ],
        out_specs=[pl.BlockSpec((rows, W), lambda n: (n, 0)), pl.BlockSpec((rows, W), lambda n: (nt - 1 - n, 0))],
        out_shape=[jax.ShapeDtypeStruct((TB, W), BF16)] * 2,
        scratch_shapes=[pltpu.VMEM((2, rows, bw.shape[3]), F32), pltpu.VMEM((2, rows, bw.shape[3]), BF16),
                        pltpu.VMEM((2, nb, nstate), F32)],
        compiler_params=_cparams(("arbitrary",)),
        name="s5_scan",
    )(utm, utm, bw, cw, a_b)


def _lru_kernel(xf_ref, xfp_ref, xfn_ref, xb_ref, xbp_ref, xbn_ref, cw_ref, cb_ref, wg_ref, bg_ref, lam_ref,
                of_ref, ob_ref, a_s, b_s, carry_ref, *, nb):
    n = pl.program_id(0)
    nt = pl.num_programs(0)
    R = xf_ref.shape[0]
    W = xf_ref.shape[1]
    tt = R // nb

    @pl.when(n == 0)
    def _():
        carry_ref[...] = jnp.zeros_like(carry_ref)

    streams = ((0, n, xf_ref, xfp_ref, xfn_ref, of_ref), (1, nt - 1 - n, xb_ref, xbp_ref, xbn_ref, ob_ref))
    for d, pos, xm_ref, xp_ref, xn_ref, o_ref in streams:
        xp = xp_ref[...].astype(F32) * (pos > 0).astype(F32)
        xn = xn_ref[...].astype(F32) * (pos < nt - 1).astype(F32)
        xe = jnp.concatenate([xp, xm_ref[...].astype(F32), xn], axis=0)
        w = cw_ref[...]
        xc = cb_ref[...] + sum(w[k:k + 1] * xe[k * nb:k * nb + R] for k in range(LRU_CONV))
        th = jnp.tanh(_dot(xc.astype(BF16), wg_ref[d]) + bg_ref[d])
        ig = 0.5 * th[:, W:] + 0.5
        c2 = (-0.5 * LRU_C) * _softplus(-lam_ref[d])
        a = jnp.exp(c2 * th[:, :W] + c2)
        a_s[d] = a
        b_s[d] = jnp.sqrt(1.0 - a * a) * (ig * xc)

        def body(s, h, d=d, o_ref=o_ref):
            t = s if d == 0 else tt - 1 - s
            rows = pl.ds(pl.multiple_of(t * nb, nb), nb)
            h = a_s[d, rows, :] * h + b_s[d, rows, :]
            o_ref[rows, :] = h.astype(o_ref.dtype)
            return h

        carry_ref[d] = lax.fori_loop(0, tt, body, carry_ref[d], unroll=4)


def _lru(utm, conv_w, conv_b, wg, bg, lam, nb, rows, col_blk):
    TB = utm.shape[0]
    W = conv_w.shape[1]
    nt = TB // rows
    hp = 2 * nb
    hn = nb
    nbt = TB // nb

    def specs(pos):
        return [
            pl.BlockSpec((rows, W), lambda n: (pos(n), col_blk)),
            pl.BlockSpec((hp, W), lambda n: (jnp.maximum(pos(n) * (rows // hp) - 1, 0), col_blk)),
            pl.BlockSpec((hn, W), lambda n: (jnp.minimum((pos(n) + 1) * (rows // hn), nbt - 1), col_blk)),
        ]

    fwd = lambda n: n
    bwd = lambda n: nt - 1 - n
    return pl.pallas_call(
        functools.partial(_lru_kernel, nb=nb),
        grid=(nt,),
        in_specs=specs(fwd) + specs(bwd) + [_const_spec(conv_w.shape), _const_spec(conv_b.shape),
                                            _const_spec(wg.shape), _const_spec(bg.shape), _const_spec(lam.shape)],
        out_specs=[pl.BlockSpec((rows, W), lambda n: (n, 0)), pl.BlockSpec((rows, W), lambda n: (nt - 1 - n, 0))],
        out_shape=[jax.ShapeDtypeStruct((TB, W), BF16)] * 2,
        scratch_shapes=[pltpu.VMEM((2, rows, W), F32), pltpu.VMEM((2, rows, W), F32), pltpu.VMEM((2, nb, W), F32)],
        compiler_params=_cparams(("arbitrary",)),
        name="lru_scan",
    )(utm, utm, utm, utm, utm, utm, conv_w, conv_b, wg, bg, lam)


def _post_kernel(u_ref, g_ref, yf_ref, yb_ref, hf_ref, hb_ref, d_ref, gw_ref, gb_ref, perm_ref, o_ref):
    nb, tt = o_ref.shape[0], o_ref.shape[1]
    y = u_ref[...].astype(F32) * d_ref[...] + yf_ref[...].astype(F32) + yb_ref[...].astype(F32)
    zg = jax.nn.gelu(y)
    s5 = zg * _sigmoid(_dot(zg.astype(BF16), gw_ref[...]) + gb_ref[...])
    h = hf_ref[...].astype(F32) + hb_ref[...].astype(F32)
    lru = h * jax.nn.gelu(g_ref[...].astype(F32))
    res = jnp.concatenate([s5, lru], axis=1).astype(BF16)
    tok = _dot(perm_ref[...], res).astype(o_ref.dtype)
    for b in range(nb):
        o_ref[b] = tok[b * tt:(b + 1) * tt]


def _post(utm, yf, yb, hf, hb, s5_d, glu_w, glu_b, nb, rows, u_blk, g_blk):
    TB, W = yf.shape
    tt = rows // nb
    r = jnp.arange(rows)
    perm = (r[None, :] == ((r % tt) * nb + r // tt)[:, None]).astype(BF16)
    row = lambda n: (n, 0)
    return pl.pallas_call(
        _post_kernel,
        grid=(TB // rows,),
        in_specs=[
            pl.BlockSpec((rows, W), lambda n: (n, u_blk)),
            pl.BlockSpec((rows, W), lambda n: (n, g_blk)),
            pl.BlockSpec((rows, W), row), pl.BlockSpec((rows, W), row),
            pl.BlockSpec((rows, W), row), pl.BlockSpec((rows, W), row),
            _const_spec(s5_d.shape), _const_spec(glu_w.shape), _const_spec(glu_b.shape), _const_spec(perm.shape),
        ],
        out_specs=pl.BlockSpec((nb, tt, 2 * W), lambda n: (0, n, 0)),
        out_shape=jax.ShapeDtypeStruct((nb, TB // nb, 2 * W), BF16),
        compiler_params=_cparams(("parallel",)),
        name="s5_lru_post",
    )(utm, utm, yf, yb, hf, hb, s5_d, glu_w, glu_b, perm)


def _merge_kernel(x_ref, of_ref, ob_ref, z_ref, sl_ref, scb_ref, scc_ref, scx_ref, ccp_ref, cxp_ref, ccn_ref,
                  cxn_ref, g0_ref, g1_ref, g2_ref, g3_ref, gn_ref, scw_ref, wb_ref, wo_ref, o_ref):
    i = pl.program_id(1)
    last = pl.num_programs(1) - 1
    tm = x_ref.shape[1]
    W = z_ref.shape[2]
    hal = ccp_ref.shape[1]

    cx = scc_ref[0].astype(F32) * scx_ref[0].astype(F32)
    cx_p = (ccp_ref[0, hal - 1:hal].astype(F32) * cxp_ref[0, hal - 1:hal].astype(F32)) * (i > 0).astype(F32)
    cx_n = (ccn_ref[0, 0:1].astype(F32) * cxn_ref[0, 0:1].astype(F32)) * (i < last).astype(F32)
    row = lax.broadcasted_iota(jnp.int32, (tm, 1), 0)
    m1 = jnp.where(row == 0, cx_p, pltpu.roll(cx, 1, 0))
    p1 = jnp.where(row == tm - 1, cx_n, pltpu.roll(cx, tm - 1, 0))
    w = scw_ref[...]
    conv = w[0:1] * m1 + w[1:2] * cx + w[2:3] * p1
    g_refs = (g0_ref, g1_ref, g2_ref, g3_ref)

    def gated_sum(rs):
        o = of_ref[0, rs, :].astype(F32) + ob_ref[0, rs, :].astype(F32)
        z = z_ref[0, rs, :].astype(F32)
        ys = []
        for h in range(GDN_HEADS):
            oh = o[:, h * LANES:(h + 1) * LANES]
            ys.append(oh * lax.rsqrt(jnp.mean(oh * oh, axis=-1, keepdims=True) + EPS) * gn_ref[...])
        y_gdn = jnp.concatenate(ys, axis=1) * (z * _sigmoid(z))
        y_sc = scb_ref[0, rs, :].astype(F32) * conv[rs]
        branches = (y_gdn.astype(BF16), sl_ref[0, rs, :W], sl_ref[0, rs, W:], y_sc.astype(BF16))
        merged = None
        for m, y in enumerate(branches):
            hd = _dot(y, wb_ref[m])
            t = hd + hd * jnp.tanh(g_refs[m][0, rs, :].astype(F32))
            merged = t if merged is None else merged + t
        return merged.astype(BF16)

    half = tm // 2
    rows = [slice(0, half), slice(half, tm)]
    merged = gated_sum(rows[0])
    for k in range(2):
        nxt = gated_sum(rows[k + 1]) if k == 0 else None
        o_ref[0, rows[k], :] = x_ref[0, rows[k], :] + _dot(merged, wo_ref[...])
        merged = nxt


def _merge(x3, o_f, o_b, P3, sl3, gdn_gain, sc_w, w_branch, w_out, tm, blk):
    Bn, S, D = x3.shape
    W = o_f.shape[2]
    hal = BF16_ROWS
    nb = tm // hal
    tok = lambda b, i: (b, i, 0)
    col = lambda k: (lambda b, i: (b, i, k))
    prev = lambda k: (lambda b, i: (b, jnp.maximum(i * nb - 1, 0), k))
    nxt = lambda k: (lambda b, i: (b, jnp.minimum((i + 1) * nb, S // hal - 1), k))
    gate0 = blk["gates"] * W // D
    return pl.pallas_call(
        _merge_kernel,
        grid=(Bn, S // tm),
        in_specs=[
            pl.BlockSpec((1, tm, D), tok),
            pl.BlockSpec((1, tm, W), tok), pl.BlockSpec((1, tm, W), tok),
            pl.BlockSpec((1, tm, W), col(blk["z"])),
            pl.BlockSpec((1, tm, 2 * W), tok),
            pl.BlockSpec((1, tm, W), col(blk["sc_b"])),
            pl.BlockSpec((1, tm, W), col(blk["sc_c"])),
            pl.BlockSpec((1, tm, W), col(blk["sc_x"])),
            pl.BlockSpec((1, hal, W), prev(blk["sc_c"])), pl.BlockSpec((1, hal, W), prev(blk["sc_x"])),
            pl.BlockSpec((1, hal, W), nxt(blk["sc_c"])), pl.BlockSpec((1, hal, W), nxt(blk["sc_x"])),
            pl.BlockSpec((1, tm, D), col(gate0)), pl.BlockSpec((1, tm, D), col(gate0 + 1)),
            pl.BlockSpec((1, tm, D), col(gate0 + 2)), pl.BlockSpec((1, tm, D), col(gate0 + 3)),
            _const_spec(gdn_gain.shape), _const_spec(sc_w.shape), _const_spec(w_branch.shape),
            _const_spec(w_out.shape),
        ],
        out_specs=pl.BlockSpec((1, tm, D), tok),
        out_shape=jax.ShapeDtypeStruct((Bn, S, D), F32),
        compiler_params=_cparams(("parallel", "parallel")),
        name="merge",
    )(x3, o_f, o_b, P3, sl3, P3, P3, P3, P3, P3, P3, P3, P3, P3, P3, P3, gdn_gain, sc_w, w_branch, w_out)


def _kv_kernel(m_ref, g_ref, w_ref, o_ref):
    o_ref[0] = _dot(_rms(m_ref[0], g_ref[...]).astype(BF16), w_ref[...]).astype(o_ref.dtype)


def _kv(mem, gain, w_kv):
    Bn, M, D = mem.shape
    return pl.pallas_call(
        _kv_kernel,
        grid=(Bn,),
        in_specs=[pl.BlockSpec((1, M, D), lambda b: (b, 0, 0)), _const_spec(gain.shape), _const_spec(w_kv.shape)],
        out_specs=pl.BlockSpec((1, M, 2 * D), lambda b: (b, 0, 0)),
        out_shape=jax.ShapeDtypeStruct((Bn, M, 2 * D), BF16),
        compiler_params=_cparams(("parallel",)),
        name="xa_kv",
    )(mem, gain, w_kv)


def _xa_kernel(x_ref, kv_ref, g_ref, wq_ref, wo_ref, o_ref):
    D = x_ref.shape[2]
    dh = D // XA_HEADS
    x = x_ref[0]
    q = _dot(_rms(x, g_ref[...]).astype(BF16), wq_ref[...]).astype(BF16)
    def scores(h):
        return _dot_nt(q[:, h * dh:(h + 1) * dh], kv_ref[0, :, h * dh:(h + 1) * dh]) * (dh ** -0.5)

    def softmax(s):
        e = jnp.exp(s - jnp.max(s, axis=-1, keepdims=True))
        return (e / jnp.sum(e, axis=-1, keepdims=True)).astype(BF16)

    def values(h, p):
        return _dot(p, kv_ref[0, :, D + h * dh:D + (h + 1) * dh])

    H = XA_HEADS
    s, p, outs = {}, {}, {}
    s[0] = scores(0)
    for h in range(H + 2):
        if h + 1 < H:
            s[h + 1] = scores(h + 1)
        if h < H:
            p[h] = softmax(s[h])
        if 1 <= h <= H:
            outs[h - 1] = values(h - 1, p[h - 1])
    o = jnp.concatenate([outs[h] for h in range(H)], axis=1).astype(BF16)
    o_ref[0] = x + _dot(o, wo_ref[...])


def _xa(x3, kv, gain, wq, wo, tm):
    Bn, S, D = x3.shape
    M = kv.shape[1]
    tok = lambda b, i: (b, i, 0)
    return pl.pallas_call(
        _xa_kernel,
        grid=(Bn, S // tm),
        in_specs=[pl.BlockSpec((1, tm, D), tok), pl.BlockSpec((1, M, 2 * D), lambda b, i: (b, 0, 0)),
                  _const_spec(gain.shape), _const_spec(wq.shape), _const_spec(wo.shape)],
        out_specs=pl.BlockSpec((1, tm, D), tok),
        out_shape=jax.ShapeDtypeStruct((Bn, S, D), F32),
        compiler_params=_cparams(("parallel", "parallel")),
        name="xattn",
    )(x3, kv, gain, wq, wo)


def _ffn_kernel(x_ref, xp_ref, xn_ref, g_ref, wu_ref, cw_ref, cb_ref, wd_ref, *rest, tf):
    fg_ref = rest[0] if len(rest) == 3 else None
    o_ref, act_ref = rest[-2:]
    i = pl.program_id(1)
    last = pl.num_programs(1) - 1
    tm = x_ref.shape[1]
    hal = xp_ref.shape[1]
    F = wd_ref.shape[0]
    x = x_ref[0]
    g = g_ref[...]
    hp = _rms(xp_ref[0], g) * (i > 0).astype(F32)
    hn = _rms(xn_ref[0], g) * (i < last).astype(F32)
    he = jnp.concatenate([hp, _rms(x, g), hn], axis=0).astype(BF16)
    ext = tm + 2 * hal

    def conv(u, cols):
        w = cw_ref[:, cols]
        m1 = pltpu.roll(u, 1, 0)[hal:hal + tm]
        p1 = pltpu.roll(u, ext - 1, 0)[hal:hal + tm]
        return w[0:1] * m1 + w[1:2] * u[hal:hal + tm] + w[2:3] * p1 + cb_ref[:, cols]

    for f in range(F // tf):
        gc = slice(f * tf, (f + 1) * tf)
        uc = slice(F + f * tf, F + (f + 1) * tf)
        gate = conv(_dot(he, wu_ref[:, gc]), gc)
        up = conv(_dot(he, wu_ref[:, uc]), uc)
        act_ref[:, gc] = (gate * _sigmoid(gate) * up).astype(BF16)
    out = x + _dot(act_ref[...], wd_ref[...])
    o_ref[0] = out if fg_ref is None else _rms(out, fg_ref[...])


def _ffn(x3, gain, w_up, conv_w, conv_b, w_down, tm, tf, final_gain=None):
    Bn, S, D = x3.shape
    hal = 8
    nb = tm // hal
    tok = lambda b, i: (b, i, 0)
    resident = lambda shape: pl.BlockSpec(shape, lambda b, i: (0, 0), pipeline_mode=pl.Buffered(1))
    return pl.pallas_call(
        functools.partial(_ffn_kernel, tf=tf),
        grid=(Bn, S // tm),
        in_specs=[
            pl.BlockSpec((1, tm, D), tok),
            pl.BlockSpec((1, hal, D), lambda b, i: (b, jnp.maximum(i * nb - 1, 0), 0)),
            pl.BlockSpec((1, hal, D), lambda b, i: (b, jnp.minimum((i + 1) * nb, S // hal - 1), 0)),
            _const_spec(gain.shape), resident(w_up.shape), _const_spec(conv_w.shape), _const_spec(conv_b.shape),
            resident(w_down.shape),
        ] + ([] if final_gain is None else [_const_spec(final_gain.shape)]),
        out_specs=pl.BlockSpec((1, tm, D), tok),
        out_shape=jax.ShapeDtypeStruct((Bn, S, D), F32),
        scratch_shapes=[pltpu.VMEM((tm, w_down.shape[0]), BF16)],
        compiler_params=_cparams(("parallel", "parallel")),
        name="ffn",
    )(x3, x3, x3, gain, w_up, conv_w, conv_b, w_down, *([] if final_gain is None else [final_gain]))


def _s5_weights(lam_re, lam_im, log_step, b_re, b_im, c_re, c_im, nb):
    G, P, J = b_re.shape[1:]
    gq = LANES // J
    nq = G // gq
    lr = jnp.minimum(lam_re, -1e-4)
    li = lam_im
    step = jnp.exp(log_step)[..., None]
    mag = jnp.exp(lr * step)
    a_re, a_im = mag * jnp.cos(li * step), mag * jnp.sin(li * step)
    den = lr * lr + li * li
    f_re = ((a_re - 1.0) * lr + a_im * li) / den
    f_im = (a_im * lr - (a_re - 1.0) * li) / den
    bb_re = f_re[..., None] * b_re - f_im[..., None] * b_im
    bb_im = f_re[..., None] * b_im + f_im[..., None] * b_re
    eye = jnp.eye(gq, dtype=F32)

    def expand(bb):
        t = bb.reshape(2, nq, gq, P, J)
        return jnp.einsum("dqgpj,gh->dqgjhp", t, eye).reshape(2, nq, gq * J, gq * P)

    def contract(cc):
        t = cc.reshape(2, nq, gq, J, P)
        return jnp.einsum("dqgjp,gh->dqgphj", t, eye).reshape(2, nq, gq * P, gq * J)

    bw = jnp.concatenate([expand(bb_re), expand(bb_im)], axis=3).astype(BF16)
    cw = jnp.concatenate([contract(c_re), -contract(c_im)], axis=2).astype(BF16)
    a_b = jnp.stack([a_re.reshape(2, G * P), a_im.reshape(2, G * P)], axis=1)
    a_b = jnp.broadcast_to(a_b[:, :, None, :], (2, 2, nb, G * P))
    return bw, cw, a_b


def _lru_gate_weights(wa, wx, ba, bx):
    nblk, bs = wa.shape[1], wa.shape[2]
    eye = jnp.eye(nblk, dtype=F32)
    dense = lambda w: jnp.einsum("dnkm,nl->dnklm", w, eye).reshape(2, nblk * bs, nblk * bs)
    wg = (0.5 * jnp.concatenate([dense(wa), dense(wx)], axis=2)).astype(BF16)
    bg = 0.5 * jnp.concatenate([ba, bx], axis=1)[:, None, :]
    return wg, bg


def _pick(n, pref):
    return pref if n % pref == 0 else n


def kernel(x, mem, mix_norm, w_in, gdn_conv, gdn_a_log, gdn_dt_bias, gdn_out_norm, s5_lambda_re, s5_lambda_im, s5_log_step, s5_b_re, s5_b_im, s5_c_re, s5_c_im, s5_d, s5_glu_w, s5_glu_b, lru_conv_w, lru_conv_b, lru_gate_a_w, lru_gate_a_b, lru_gate_x_w, lru_gate_x_b, lru_lambda, sc_conv, w_branch, w_mix_out, xa_norm, xa_mem_norm, xa_w_q, xa_w_kv, xa_w_o, ffn_norm, ffn_w_up, ffn_conv_w, ffn_conv_b, ffn_w_down, final_norm):
    Bn, S, D = x.shape
    depth = w_in.shape[0]
    W = D // 2
    H = GDN_HEADS
    T = Bn * S
    F = ffn_w_down.shape[1]
    assert Bn % BF16_ROWS == 0 and S % 256 == 0 and W == H * LANES

    blk = {"z": 3, "s5_u": 4, "lru_x": 5, "lru_g": 6, "sc_b": 7, "sc_c": 8, "sc_x": 9, "gates": 10}
    n_ab = 4 * H
    c_ab = 4 * W
    tm_proj = _pick(S, 1024)
    ts_tok = _pick(S, 512)
    ts_big = _pick(S, 1024)
    ct_gdn = _pick(S, 4 * GDN_HEADS * GDN_CHUNK)
    rows_post = 32 * Bn
    rows_scan = (64 if S % 64 == 0 else 32) * Bn

    row2 = lambda v: v.reshape(1, -1)
    for l in range(depth):
        c_gate = c_ab + n_ab + 6 * W
        w_main = jnp.concatenate([w_in[l][:, :c_ab], w_in[l][:, c_ab + n_ab:c_gate], 0.5 * w_in[l][:, c_gate:]],
                                 axis=1).astype(BF16)
        w_ab = jnp.pad(w_in[l][:, c_ab:c_ab + n_ab], ((0, 0), (0, LANES - n_ab))).astype(BF16)
        P, ab = _proj(x.reshape(T, D), row2(mix_norm[l]), w_main, w_ab, tm_proj, _pick(w_main.shape[1], 3072))
        P3 = P.reshape(Bn, S, -1)

        qkv = _qkvprep(P3, 0.5 * gdn_conv[l], ts_tok)
        prm = jnp.zeros((8, LANES), F32)
        prm = prm.at[0, 2 * H:4 * H].set(gdn_a_log[l].reshape(-1)).at[1, 2 * H:4 * H].set(gdn_dt_bias[l].reshape(-1))
        o_f, o_b = _gdn(qkv, ab.reshape(Bn, S, LANES), prm, ct_gdn, 3 * W)

        utm = jnp.swapaxes(P3[:, :, blk["s5_u"] * W:(blk["lru_g"] + 1) * W], 0, 1).reshape(S * Bn, 3 * W)
        bw, cw, a_b = _s5_weights(s5_lambda_re[l], s5_lambda_im[l], s5_log_step[l], s5_b_re[l], s5_b_im[l],
                                  s5_c_re[l], s5_c_im[l], Bn)
        y_f, y_b = _s5(utm, bw, cw, a_b, Bn, rows_scan, 0)
        wg, bg = _lru_gate_weights(lru_gate_a_w[l], lru_gate_x_w[l], lru_gate_a_b[l], lru_gate_x_b[l])
        h_f, h_b = _lru(utm, lru_conv_w[l], row2(lru_conv_b[l]), wg, bg, lru_lambda[l][:, None, :], Bn, rows_scan, 1)
        sl3 = _post(utm, y_f, y_b, h_f, h_b, row2(s5_d[l]), s5_glu_w[l].astype(BF16), row2(s5_glu_b[l]),
                    Bn, rows_post, 0, 2)

        x = _merge(x, o_f, o_b, P3, sl3, row2(gdn_out_norm[l]), sc_conv[l], (0.5 * w_branch[l]).astype(BF16),
                   w_mix_out[l].astype(BF16), ts_tok, blk)

        kv = _kv(mem, row2(xa_mem_norm[l]), xa_w_kv[l].astype(BF16))
        x = _xa(x, kv, row2(xa_norm[l]), xa_w_q[l].astype(BF16), xa_w_o[l].astype(BF16), ts_big)

        x = _ffn(x, row2(ffn_norm[l]), ffn_w_up[l].astype(BF16), ffn_conv_w[l], row2(ffn_conv_b[l]),
                 ffn_w_down[l].astype(BF16), ts_big, 256, row2(final_norm) if l == depth - 1 else None)
    return x
```
